```python
import math
import jax, jax.numpy as jnp
from jax import lax
import numpy as np

D_MODEL = 1024
BATCH = 16
SEQ = 4096
DEPTH = 1

N_META = 16
E_CONV = D_MODEL
K_CONV_A = 3
E_LRU = D_MODEL
LRU_HEAD_DIM = 256
N_LRU_HEADS = E_LRU // LRU_HEAD_DIM
K_CONV_B = 4
C_LRU = 8.0
RMS_EPS = 1e-6
N_IN = 4 * E_CONV + 2 * E_LRU + 2 * D_MODEL

kernel_name = "hybrid_shortconv_rglru_gated_merge"


def rms_norm(x, g):
    xf = x.astype(jnp.float32)
    y = xf * lax.rsqrt(jnp.mean(xf * xf, axis=-1, keepdims=True) + RMS_EPS)
    return (y * g.astype(jnp.float32)).astype(x.dtype)


def causal_dwconv(x, w):
    k_len = w.shape[0]
    s = x.shape[1]
    xp = jnp.pad(x, ((0, 0), (k_len - 1, 0), (0, 0)))
    y = xp[:, 0:s] * w[0]
    for k in range(1, k_len):
        y = y + xp[:, k:k + s] * w[k]
    return y


def rg_lru(x, w_a, b_a, w_i, b_i, lam):
    bn, s, e = x.shape
    xf = x.astype(jnp.float32)
    xh = xf.reshape(bn, s, N_LRU_HEADS, LRU_HEAD_DIM)
    r = jax.nn.sigmoid(jnp.einsum('bshi,hij->bshj', xh, w_a.astype(jnp.float32)) + b_a.astype(jnp.float32)).reshape(bn, s, e)
    i = jax.nn.sigmoid(jnp.einsum('bshi,hij->bshj', xh, w_i.astype(jnp.float32)) + b_i.astype(jnp.float32)).reshape(bn, s, e)
    log_a = -C_LRU * r * jax.nn.softplus(-lam.astype(jnp.float32))
    a = jnp.exp(log_a)
    mult = jnp.sqrt(-jnp.expm1(2.0 * log_a))
    is_first = (jnp.arange(s) == 0)[None, :, None]
    mult = jnp.where(is_first, 1.0, mult)
    u = xf * i * mult

    def combine(left, right):
        a_l, b_l = left
        a_r, b_r = right
        return a_l * a_r, a_r * b_l + b_r

    _, h = lax.associative_scan(combine, (a, u), axis=1)
    return h.astype(x.dtype)


def setup_inputs(seed: int = 0) -> dict:
    key = jax.random.key(seed)
    ks = jax.random.split(key, 18)
    f32 = jnp.float32
    nrm = lambda k, shape, scale: jax.random.normal(k, shape, f32) * scale
    x = jax.random.normal(ks[0], (BATCH, SEQ, D_MODEL), f32)
    meta = nrm(ks[1], (N_META, D_MODEL), 1.0)
    norm_g = 1.0 + nrm(ks[2], (DEPTH, D_MODEL), 0.02)
    w_in = nrm(ks[3], (DEPTH, D_MODEL, N_IN), D_MODEL ** -0.5)
    b_gate = nrm(ks[4], (DEPTH, 2 * D_MODEL), 0.02)
    conv_a_w = nrm(ks[5], (DEPTH, K_CONV_A, E_CONV), K_CONV_A ** -0.5)
    w_proj_a = nrm(ks[6], (DEPTH, E_CONV, D_MODEL), E_CONV ** -0.5)
    conv_b_w = nrm(ks[7], (DEPTH, K_CONV_B, E_LRU), K_CONV_B ** -0.5)
    conv_b_b = nrm(ks[8], (DEPTH, E_LRU), 0.02)
    w_rg_a = nrm(ks[9], (DEPTH, N_LRU_HEADS, LRU_HEAD_DIM, LRU_HEAD_DIM), LRU_HEAD_DIM ** -0.5)
    b_rg_a = nrm(ks[10], (DEPTH, N_LRU_HEADS, LRU_HEAD_DIM), 0.02)
    w_rg_i = nrm(ks[11], (DEPTH, N_LRU_HEADS, LRU_HEAD_DIM, LRU_HEAD_DIM), LRU_HEAD_DIM ** -0.5)
    b_rg_i = nrm(ks[12], (DEPTH, N_LRU_HEADS, LRU_HEAD_DIM), 0.02)
    a_c = jax.random.uniform(ks[13], (DEPTH, E_LRU), f32, 0.9, 0.999)
    s_base = a_c ** (1.0 / C_LRU)
    lru_param = jnp.log(s_base) - jnp.log1p(-s_base)
    w_proj_b = nrm(ks[14], (DEPTH, E_LRU, D_MODEL), E_LRU ** -0.5)
    w_out = nrm(ks[15], (DEPTH, D_MODEL, D_MODEL), D_MODEL ** -0.5)
    final_norm_g = 1.0 + nrm(ks[16], (D_MODEL,), 0.02)
    return {"x": x, "meta": meta, "norm_g": norm_g, "w_in": w_in, "b_gate": b_gate,
            "conv_a_w": conv_a_w, "w_proj_a": w_proj_a, "conv_b_w": conv_b_w, "conv_b_b": conv_b_b,
            "w_rg_a": w_rg_a, "b_rg_a": b_rg_a, "w_rg_i": w_rg_i, "b_rg_i": b_rg_i,
            "lru_param": lru_param, "w_proj_b": w_proj_b, "w_out": w_out, "final_norm_g": final_norm_g}


def reference(x, meta, norm_g, w_in, b_gate, conv_a_w, w_proj_a, conv_b_w, conv_b_b,
              w_rg_a, b_rg_a, w_rg_i, b_rg_i, lru_param, w_proj_b, w_out, final_norm_g):
    bn = x.shape[0]
    meta_b = jnp.broadcast_to(meta[None].astype(x.dtype), (bn, N_META, D_MODEL))
    h = jnp.concatenate([meta_b, x], axis=1)
    split_idx = [E_CONV, 2 * E_CONV, 3 * E_CONV, 4 * E_CONV,
                 4 * E_CONV + E_LRU, 4 * E_CONV + 2 * E_LRU, 4 * E_CONV + 2 * E_LRU + D_MODEL]
    for l in range(DEPTH):
        xn = rms_norm(h, norm_g[l])
        proj = xn @ w_in[l]
        b_a, c_a, h_a, z_a, x_b, z_b, g_a, g_b = jnp.split(proj, split_idx, axis=-1)
        y_a = b_a * causal_dwconv(c_a * h_a, conv_a_w[l])
        y_a = (y_a * jax.nn.silu(z_a)) @ w_proj_a[l]
        x_c = causal_dwconv(x_b, conv_b_w[l]) + conv_b_b[l]
        y_b = rg_lru(x_c, w_rg_a[l], b_rg_a[l], w_rg_i[l], b_rg_i[l], lru_param[l])
        y_b = (y_b * jax.nn.silu(z_b)) @ w_proj_b[l]
        gate_a = jax.nn.sigmoid(g_a + b_gate[l, :D_MODEL])
        gate_b = jax.nn.sigmoid(g_b + b_gate[l, D_MODEL:])
        merged = gate_a * y_a + gate_b * y_b
        h = h + merged @ w_out[l]
    out = rms_norm(h, final_norm_g)
    return out[:, N_META:]
```

```python
import jax
import jax.numpy as jnp
from jax import lax
from jax.experimental import pallas as pl
from jax.experimental.pallas import tpu as pltpu

N_META = 16
N_LRU_HEADS = 4
C_LRU = 8.0
RMS_EPS = 1e-6

V7X_F32_SUBLANES = 8
V7X_VMEM_BYTES = 64 * 1024 * 1024

NB = V7X_F32_SUBLANES
SLAB_T = 2
SLAB = SLAB_T * NB
TT = 32
N_SPLITS = 8

P_NORM_G, P_CONV_A, P_CONV_B, P_CONV_B_BIAS = 0, 1, 4, 8
P_B_RG_A, P_B_RG_I, P_LAM, P_BGATE_A, P_BGATE_B, P_FINAL_G = 9, 10, 11, 12, 13, 14
N_PARAM_ROWS = 15

S_BA, S_CA, S_HA, S_ZA, S_XB, S_ZB, S_GA, S_GB = range(N_SPLITS)

F32 = jnp.float32
BF16 = jnp.bfloat16


def _sigmoid(v):
    return 1.0 / (1.0 + jnp.exp(-v))


def _rms_norm(v, g):
    ms = jnp.mean(v * v, axis=-1, keepdims=True)
    return v * lax.rsqrt(ms + RMS_EPS) * g


def _slabs(n_rows):
    return [slice(j * SLAB, (j + 1) * SLAB) for j in range(n_rows // SLAB)]


def _dot(a, b):
    return jnp.dot(a, b, preferred_element_type=F32)


def _conv_b_phase(xb_ref, hist_b, p_ref, xc_buf, xcb_buf):
    cb = [p_ref[P_CONV_B + k] for k in range(4)]
    bias = p_ref[P_CONV_B_BIAS]
    p2h = hist_b[:NB, :]
    p1 = hist_b[NB:, :]
    for sl in _slabs(xb_ref.shape[0]):
        xb = xb_ref[sl, :]
        tap0 = jnp.concatenate([p2h, p1[:NB]], axis=0)
        tap2 = jnp.concatenate([p1[NB:], xb[:NB]], axis=0)
        xc = cb[0] * tap0 + cb[1] * p1 + cb[2] * tap2 + cb[3] * xb + bias
        xc_buf[sl, :] = xc
        xcb_buf[sl, :] = xc.astype(BF16)
        p2h = p1[NB:]
        p1 = xb
    hist_b[:NB, :] = p2h
    hist_b[NB:, :] = p1


def _gate_dots(xcb_buf, w_ra, w_ri, r_buf, i_buf):
    hd = xcb_buf.shape[1] // N_LRU_HEADS
    for h in range(N_LRU_HEADS):
        cs = slice(h * hd, (h + 1) * hd)
        r_buf[:, cs] = _dot(xcb_buf[:, cs], w_ra[h])
        i_buf[:, cs] = _dot(xcb_buf[:, cs], w_ri[h])


def _lru_phase(r_buf, i_buf, xc_buf, p_ref, hstate, first, zb_ref=None, yb_in=None):
    b_ra = p_ref[P_B_RG_A]
    b_ri = p_ref[P_B_RG_I]
    z = -p_ref[P_LAM]
    softplus = jnp.maximum(z, 0.0) + jnp.log1p(jnp.exp(-jnp.abs(z)))
    cs = -C_LRU * softplus
    h = hstate[...]
    for j, sl in enumerate(_slabs(r_buf.shape[0])):
        r = _sigmoid(r_buf[sl, :] + b_ra)
        ig = _sigmoid(i_buf[sl, :] + b_ri)
        log_a = cs * r
        a = jnp.exp(log_a)
        mult = jnp.sqrt(1.0 - a * a)
        if first and j == 0:
            mult = jnp.concatenate([jnp.ones((NB, mult.shape[1]), F32), mult[NB:]], axis=0)
        u = xc_buf[sl, :] * ig * mult
        h0 = a[:NB] * h + u[:NB]
        h = a[NB:] * h0 + u[NB:]
        if yb_in is not None:
            zb = zb_ref[sl, :]
            hs = jnp.concatenate([h0, h], axis=0)
            yb_in[sl, :] = (hs * (zb * _sigmoid(zb))).astype(BF16)
    hstate[...] = h


def _meta_kernel(xm_ref, p_ref, w_ca, w_ha, w_xb, w_ra, w_ri,
                 sta_ref, stb_ref, sth_ref,
                 xn_buf, pc_buf, ph_buf, px_buf, xc_buf, xcb_buf, r_buf, i_buf, hist_b, hstate):
    m = xm_ref.shape[0]
    g_norm = p_ref[P_NORM_G]
    for sl in _slabs(m):
        xn_buf[sl, :] = _rms_norm(xm_ref[sl, :], g_norm).astype(BF16)
    pc_buf[...] = _dot(xn_buf[...], w_ca[...])
    ph_buf[...] = _dot(xn_buf[...], w_ha[...])
    px_buf[...] = _dot(xn_buf[...], w_xb[...])
    last = slice(m - SLAB, m)
    sta_ref[...] = pc_buf[last, :] * ph_buf[last, :]
    hist_b[...] = jnp.zeros_like(hist_b)
    hstate[...] = jnp.zeros_like(hstate)
    _conv_b_phase(px_buf, hist_b, p_ref, xc_buf, xcb_buf)
    _gate_dots(xcb_buf, w_ra, w_ri, r_buf, i_buf)
    _lru_phase(r_buf, i_buf, xc_buf, p_ref, hstate, first=True)
    stb_ref[...] = hist_b[...]
    sth_ref[...] = hstate[...]


def _main_kernel(x_hbm, sta_ref, stb_ref, sth_ref, p_ref, w_in, w_pa, w_pb, w_o, w_ra, w_ri,
                 o_hbm,
                 xbuf, obuf, sem_in, sem_out, hist_a, hist_b, hstate,
                 xn_buf, proj, ya_in, xc_buf, xcb_buf, r_buf, i_buf, yb_in, ya_buf, yb_buf, m_buf, o_buf):
    g = pl.program_id(0)
    c = pl.program_id(1)
    nc = pl.num_programs(1)
    step = g * nc + c
    n_steps = pl.num_programs(0) * nc
    slot = lax.rem(c, 2)
    d = xn_buf.shape[1]

    def in_copies(gg, cc, s):
        return [pltpu.make_async_copy(x_hbm.at[gg * NB + b, pl.ds(cc * TT, TT), :],
                                      xbuf.at[s, :, b, :], sem_in.at[s]) for b in range(NB)]

    def out_copies(gg, cc, s):
        return [pltpu.make_async_copy(obuf.at[s, :, b, :],
                                      o_hbm.at[gg * NB + b, pl.ds(cc * TT, TT), :], sem_out.at[s]) for b in range(NB)]

    @pl.when(step == 0)
    def _():
        for cp in in_copies(g, c, slot):
            cp.start()

    @pl.when(step + 1 < n_steps)
    def _():
        wrap = c + 1 == nc
        for cp in in_copies(jnp.where(wrap, g + 1, g), jnp.where(wrap, 0, c + 1), 1 - slot):
            cp.start()

    for cp in in_copies(g, c, slot):
        cp.wait()

    @pl.when(step >= 2)
    def _():
        back = c < 2
        for cp in out_copies(jnp.where(back, g - 1, g), jnp.where(back, c - 2 + nc, c - 2), slot):
            cp.wait()

    @pl.when(c == 0)
    def _():
        hist_a[...] = sta_ref[...]
        hist_b[...] = stb_ref[...]
        hstate[...] = sth_ref[...]

    xs = xbuf.at[slot]
    os_ = obuf.at[slot]
    m = TT * NB
    slabs = _slabs(m)

    def x_slab(j):
        return xs[pl.ds(j * SLAB_T, SLAB_T)].reshape(SLAB, d)

    def in_proj(k):
        proj[k] = _dot(xn_buf[...], w_in[:, k * d:(k + 1) * d])

    g_norm = p_ref[P_NORM_G]
    for j, sl in enumerate(slabs):
        xn_buf[sl, :] = _rms_norm(x_slab(j), g_norm).astype(BF16)

    for k in (S_CA, S_HA, S_BA, S_ZA):
        in_proj(k)
    ca = [p_ref[P_CONV_A + k] for k in range(3)]
    prev = hist_a[...]
    for sl in slabs:
        ch = proj[S_CA, sl, :] * proj[S_HA, sl, :]
        tap1 = jnp.concatenate([prev[NB:], ch[:NB]], axis=0)
        conv = ca[0] * prev + ca[1] * tap1 + ca[2] * ch
        za = proj[S_ZA, sl, :]
        ya_in[sl, :] = (proj[S_BA, sl, :] * conv * (za * _sigmoid(za))).astype(BF16)
        prev = ch
    hist_a[...] = prev
    ya_buf[...] = _dot(ya_in[...], w_pa[...])

    in_proj(S_XB)
    _conv_b_phase(proj.at[S_XB], hist_b, p_ref, xc_buf, xcb_buf)
    _gate_dots(xcb_buf, w_ra, w_ri, r_buf, i_buf)
    in_proj(S_ZB)
    _lru_phase(r_buf, i_buf, xc_buf, p_ref, hstate, first=False, zb_ref=proj.at[S_ZB], yb_in=yb_in)
    yb_buf[...] = _dot(yb_in[...], w_pb[...])

    in_proj(S_GA)
    in_proj(S_GB)
    bga = p_ref[P_BGATE_A]
    bgb = p_ref[P_BGATE_B]
    for sl in slabs:
        ga = _sigmoid(proj[S_GA, sl, :] + bga)
        gb = _sigmoid(proj[S_GB, sl, :] + bgb)
        m_buf[sl, :] = (ga * ya_buf[sl, :] + gb * yb_buf[sl, :]).astype(BF16)
    o_buf[...] = _dot(m_buf[...], w_o[...])
    g_fin = p_ref[P_FINAL_G]
    for j, sl in enumerate(slabs):
        out = _rms_norm(x_slab(j) + o_buf[sl, :], g_fin)
        os_[pl.ds(j * SLAB_T, SLAB_T)] = out.reshape(SLAB_T, NB, d)

    for cp in out_copies(g, c, slot):
        cp.start()

    @pl.when(step == n_steps - 1)
    def _():
        for cp in out_copies(g, c - 1, 1 - slot):
            cp.wait()
        for cp in out_copies(g, c, slot):
            cp.wait()


def _const_spec(shape):
    nd = len(shape)
    return pl.BlockSpec(shape, lambda *_: (0,) * nd, pipeline_mode=pl.Buffered(1))


def kernel(x, meta, norm_g, w_in, b_gate, conv_a_w, w_proj_a, conv_b_w, conv_b_b, w_rg_a, b_rg_a, w_rg_i, b_rg_i, lru_param, w_proj_b, w_out, final_norm_g):
    bsz, seq, d = x.shape
    assert norm_g.shape[0] == 1, "single-layer trunk only"
    assert bsz % NB == 0 and seq % (2 * TT) == 0 and N_META % SLAB_T == 0
    assert w_in.shape[2] == N_SPLITS * d

    rows = [norm_g[0], *conv_a_w[0], *conv_b_w[0], conv_b_b[0], b_rg_a[0].reshape(d), b_rg_i[0].reshape(d),
            lru_param[0], b_gate[0, :d], b_gate[0, d:], final_norm_g]
    params = jnp.broadcast_to(jnp.stack(rows)[:, None, :], (N_PARAM_ROWS, SLAB, d)).astype(F32)
    w_in_b = w_in[0].astype(BF16)
    w_pa_b = w_proj_a[0].astype(BF16)
    w_pb_b = w_proj_b[0].astype(BF16)
    w_o_b = w_out[0].astype(BF16)
    w_ra_b = w_rg_a[0].astype(BF16)
    w_ri_b = w_rg_i[0].astype(BF16)
    xm = jnp.repeat(meta.astype(F32), NB, axis=0)

    mm = N_META * NB
    col = lambda k: pl.BlockSpec((d, d), lambda i, k=k: (0, k))
    whole = lambda shape: pl.BlockSpec(shape, lambda i: (0,) * len(shape))
    st_a, st_b, st_h = pl.pallas_call(
        _meta_kernel,
        grid=(1,),
        in_specs=[whole((mm, d)), whole(params.shape), col(S_CA), col(S_HA), col(S_XB),
                  whole(w_ra_b.shape), whole(w_ri_b.shape)],
        out_specs=[whole((SLAB, d)), whole((3 * NB, d)), whole((NB, d))],
        out_shape=[jax.ShapeDtypeStruct((SLAB, d), F32), jax.ShapeDtypeStruct((3 * NB, d), F32),
                   jax.ShapeDtypeStruct((NB, d), F32)],
        scratch_shapes=[pltpu.VMEM((mm, d), BF16)] + [pltpu.VMEM((mm, d), F32)] * 4
                       + [pltpu.VMEM((mm, d), BF16)] + [pltpu.VMEM((mm, d), F32)] * 2
                       + [pltpu.VMEM((3 * NB, d), F32), pltpu.VMEM((NB, d), F32)],
        name="meta_state",
    )(xm, params, w_in_b, w_in_b, w_in_b, w_ra_b, w_ri_b)

    m = TT * NB
    act_f32 = pltpu.VMEM((m, d), F32)
    act_bf16 = pltpu.VMEM((m, d), BF16)
    return pl.pallas_call(
        _main_kernel,
        grid=(bsz // NB, seq // TT),
        in_specs=[pl.BlockSpec(memory_space=pl.ANY),
                  _const_spec((SLAB, d)), _const_spec((3 * NB, d)), _const_spec((NB, d)),
                  _const_spec(params.shape), _const_spec(w_in_b.shape), _const_spec(w_pa_b.shape),
                  _const_spec(w_pb_b.shape), _const_spec(w_o_b.shape), _const_spec(w_ra_b.shape),
                  _const_spec(w_ri_b.shape)],
        out_specs=pl.BlockSpec(memory_space=pl.ANY),
        out_shape=jax.ShapeDtypeStruct((bsz, seq, d), x.dtype),
        scratch_shapes=[
            pltpu.VMEM((2, TT, NB, d), F32),
            pltpu.VMEM((2, TT, NB, d), F32),
            pltpu.SemaphoreType.DMA((2,)),
            pltpu.SemaphoreType.DMA((2,)),
            pltpu.VMEM((SLAB, d), F32),
            pltpu.VMEM((3 * NB, d), F32),
            pltpu.VMEM((NB, d), F32),
            act_bf16,
            pltpu.VMEM((N_SPLITS, m, d), F32),
            act_bf16,
            act_f32, act_bf16,
            act_f32, act_f32,
            act_bf16,
            act_f32, act_f32,
            act_bf16,
            act_f32,
        ],
        compiler_params=pltpu.CompilerParams(
            dimension_semantics=("arbitrary", "arbitrary"),
            vmem_limit_bytes=V7X_VMEM_BYTES * 7 // 8,
        ),
        name="mixer_block",
    )(x, st_a, st_b, st_h, params, w_in_b, w_pa_b, w_pb_b, w_o_b, w_ra_b, w_ri_b)
```

```python
import jax
import jax.numpy as jnp
from jax import lax
from jax.experimental import pallas as pl
from jax.experimental.pallas import tpu as pltpu

N_META = 16
N_LRU_HEADS = 4
C_LRU = 8.0
RMS_EPS = 1e-6
SQRT_GUARD = 1e-30

V7X_F32_SUBLANES = 8
V7X_VMEM_BYTES = 64 * 1024 * 1024

NB = V7X_F32_SUBLANES
SLAB_T = 2
SLAB = SLAB_T * NB
TT = 32
N_SPLITS = 8

P_NORM_G, P_CONV_A, P_CONV_B, P_CONV_B_BIAS = 0, 1, 4, 8
P_B_RG_A, P_B_RG_I, P_LAM, P_BGATE_A, P_BGATE_B, P_FINAL_G = 9, 10, 11, 12, 13, 14
N_PARAM_ROWS = 15

S_BA, S_CA, S_HA, S_ZA, S_XB, S_ZB, S_GA, S_GB = range(N_SPLITS)

F32 = jnp.float32
BF16 = jnp.bfloat16


def _sigmoid(v):
    return 1.0 / (1.0 + jnp.exp(-v))


def _rms_norm(v, g):
    ms = jnp.mean(v * v, axis=-1, keepdims=True)
    return v * lax.rsqrt(ms + RMS_EPS) * g


def _slabs(n_rows):
    return [slice(j * SLAB, (j + 1) * SLAB) for j in range(n_rows // SLAB)]


def _dot(a, w_words):
    return jnp.dot(a, pltpu.bitcast(w_words, BF16), preferred_element_type=F32)


def _conv_b_phase(xb_ref, hist_b, p_ref, xc_buf, xcb_buf):
    cb = [p_ref[P_CONV_B + k] for k in range(4)]
    bias = p_ref[P_CONV_B_BIAS]
    p2h = hist_b[:NB, :]
    p1 = hist_b[NB:, :]
    for sl in _slabs(xb_ref.shape[0]):
        xb = xb_ref[sl, :]
        tap0 = jnp.concatenate([p2h, p1[:NB]], axis=0)
        tap2 = jnp.concatenate([p1[NB:], xb[:NB]], axis=0)
        xc = cb[0] * tap0 + cb[1] * p1 + cb[2] * tap2 + cb[3] * xb + bias
        xc_buf[sl, :] = xc
        xcb_buf[sl, :] = xc.astype(BF16)
        p2h = p1[NB:]
        p1 = xb
    hist_b[:NB, :] = p2h
    hist_b[NB:, :] = p1


def _gate_dots(xcb_buf, w_ra, w_ri, r_buf, i_buf):
    hd = xcb_buf.shape[1] // N_LRU_HEADS
    for h in range(N_LRU_HEADS):
        cs = slice(h * hd, (h + 1) * hd)
        r_buf[:, cs] = _dot(xcb_buf[:, cs], w_ra[h])
        i_buf[:, cs] = _dot(xcb_buf[:, cs], w_ri[h])


def _lru_phase(r_buf, i_buf, xc_buf, p_ref, hstate, first, zb_ref=None, yb_in=None):
    b_ra = p_ref[P_B_RG_A]
    b_ri = p_ref[P_B_RG_I]
    z = -p_ref[P_LAM]
    softplus = jnp.maximum(z, 0.0) + jnp.log1p(jnp.exp(-jnp.abs(z)))
    cs = -C_LRU * softplus
    h = hstate[...]
    for j, sl in enumerate(_slabs(r_buf.shape[0])):
        r = _sigmoid(r_buf[sl, :] + b_ra)
        ig = _sigmoid(i_buf[sl, :] + b_ri)
        log_a = cs * r
        a = jnp.exp(log_a)
        y = 1.0 - a * a
        mult = y * lax.rsqrt(jnp.maximum(y, SQRT_GUARD))
        if first and j == 0:
            mult = jnp.concatenate([jnp.ones((NB, mult.shape[1]), F32), mult[NB:]], axis=0)
        u = xc_buf[sl, :] * ig * mult
        h0 = a[:NB] * h + u[:NB]
        h = a[NB:] * h0 + u[NB:]
        if yb_in is not None:
            zb = zb_ref[sl, :]
            hs = jnp.concatenate([h0, h], axis=0)
            yb_in[sl, :] = (hs * (zb * _sigmoid(zb))).astype(BF16)
    hstate[...] = h


def _meta_kernel(xm_ref, p_ref, w_ca, w_ha, w_xb, w_ra, w_ri,
                 sta_ref, stb_ref, sth_ref,
                 xn_buf, pc_buf, ph_buf, px_buf, xc_buf, xcb_buf, r_buf, i_buf, hist_b, hstate):
    m = xm_ref.shape[0]
    g_norm = p_ref[P_NORM_G]
    for sl in _slabs(m):
        xn_buf[sl, :] = _rms_norm(xm_ref[sl, :], g_norm).astype(BF16)
    pc_buf[...] = _dot(xn_buf[...], w_ca[...])
    ph_buf[...] = _dot(xn_buf[...], w_ha[...])
    px_buf[...] = _dot(xn_buf[...], w_xb[...])
    last = slice(m - SLAB, m)
    sta_ref[...] = pc_buf[last, :] * ph_buf[last, :]
    hist_b[...] = jnp.zeros_like(hist_b)
    hstate[...] = jnp.zeros_like(hstate)
    _conv_b_phase(px_buf, hist_b, p_ref, xc_buf, xcb_buf)
    _gate_dots(xcb_buf, w_ra, w_ri, r_buf, i_buf)
    _lru_phase(r_buf, i_buf, xc_buf, p_ref, hstate, first=True)
    stb_ref[...] = hist_b[...]
    sth_ref[...] = hstate[...]


def _main_kernel(x_hbm, sta_ref, stb_ref, sth_ref, p_ref, w_in, w_pa, w_pb, w_o, w_ra, w_ri,
                 o_hbm,
                 xbuf, obuf, sem_in, sem_out, hist_a, hist_b, hstate,
                 xn_buf, proj, ya_in, xc_buf, xcb_buf, r_buf, i_buf, yb_in, ya_buf, yb_buf, m_buf, o_buf):
    g = pl.program_id(0)
    c = pl.program_id(1)
    nc = pl.num_programs(1)
    step = g * nc + c
    n_steps = pl.num_programs(0) * nc
    slot = lax.rem(c, 2)
    d = xn_buf.shape[1]

    def in_copies(gg, cc, s):
        return [pltpu.make_async_copy(x_hbm.at[gg * NB + b, pl.ds(cc * TT, TT), :],
                                      xbuf.at[s, :, b, :], sem_in.at[s]) for b in range(NB)]

    def out_copies(gg, cc, s):
        return [pltpu.make_async_copy(obuf.at[s, :, b, :],
                                      o_hbm.at[gg * NB + b, pl.ds(cc * TT, TT), :], sem_out.at[s]) for b in range(NB)]

    @pl.when(step == 0)
    def _():
        for cp in in_copies(g, c, slot):
            cp.start()

    @pl.when(step + 1 < n_steps)
    def _():
        wrap = c + 1 == nc
        for cp in in_copies(jnp.where(wrap, g + 1, g), jnp.where(wrap, 0, c + 1), 1 - slot):
            cp.start()

    for cp in in_copies(g, c, slot):
        cp.wait()

    @pl.when(step >= 2)
    def _():
        back = c < 2
        for cp in out_copies(jnp.where(back, g - 1, g), jnp.where(back, c - 2 + nc, c - 2), slot):
            cp.wait()

    @pl.when(c == 0)
    def _():
        hist_a[...] = sta_ref[...]
        hist_b[...] = stb_ref[...]
        hstate[...] = sth_ref[...]

    xs = xbuf.at[slot]
    os_ = obuf.at[slot]
    m = TT * NB
    slabs = _slabs(m)

    def x_slab(j):
        return jnp.concatenate([xs[j * SLAB_T + t] for t in range(SLAB_T)], axis=0)

    def in_proj(k):
        proj[k] = _dot(xn_buf[...], w_in[:, k * d:(k + 1) * d])

    g_norm = p_ref[P_NORM_G]
    for j, sl in enumerate(slabs):
        xn_buf[sl, :] = _rms_norm(x_slab(j), g_norm).astype(BF16)

    for k in (S_CA, S_HA, S_BA, S_ZA):
        in_proj(k)
    ca = [p_ref[P_CONV_A + k] for k in range(3)]
    prev = hist_a[...]
    for sl in slabs:
        ch = proj[S_CA, sl, :] * proj[S_HA, sl, :]
        tap1 = jnp.concatenate([prev[NB:], ch[:NB]], axis=0)
        conv = ca[0] * prev + ca[1] * tap1 + ca[2] * ch
        za = proj[S_ZA, sl, :]
        ya_in[sl, :] = (proj[S_BA, sl, :] * conv * (za * _sigmoid(za))).astype(BF16)
        prev = ch
    hist_a[...] = prev
    ya_buf[...] = _dot(ya_in[...], w_pa[...])

    in_proj(S_XB)
    _conv_b_phase(proj.at[S_XB], hist_b, p_ref, xc_buf, xcb_buf)
    _gate_dots(xcb_buf, w_ra, w_ri, r_buf, i_buf)
    in_proj(S_ZB)
    _lru_phase(r_buf, i_buf, xc_buf, p_ref, hstate, first=False, zb_ref=proj.at[S_ZB], yb_in=yb_in)
    yb_buf[...] = _dot(yb_in[...], w_pb[...])

    in_proj(S_GA)
    in_proj(S_GB)
    bga = p_ref[P_BGATE_A]
    bgb = p_ref[P_BGATE_B]
    for sl in slabs:
        ga = _sigmoid(proj[S_GA, sl, :] + bga)
        gb = _sigmoid(proj[S_GB, sl, :] + bgb)
        m_buf[sl, :] = (ga * ya_buf[sl, :] + gb * yb_buf[sl, :]).astype(BF16)
    o_buf[...] = _dot(m_buf[...], w_o[...])
    g_fin = p_ref[P_FINAL_G]
    for j, sl in enumerate(slabs):
        out = _rms_norm(x_slab(j) + o_buf[sl, :], g_fin)
        for t in range(SLAB_T):
            os_[j * SLAB_T + t] = out[t * NB:(t + 1) * NB]

    for cp in out_copies(g, c, slot):
        cp.start()

    @pl.when(step == n_steps - 1)
    def _():
        for cp in out_copies(g, c - 1, 1 - slot):
            cp.wait()
        for cp in out_copies(g, c, slot):
            cp.wait()


def _pack_rows(w):
    wb = w.astype(BF16)
    *lead, k, n = wb.shape
    return lax.bitcast_convert_type(jnp.swapaxes(wb.reshape(*lead, k // 2, 2, n), -1, -2), jnp.uint32)


def _const_spec(shape):
    nd = len(shape)
    return pl.BlockSpec(shape, lambda *_: (0,) * nd, pipeline_mode=pl.Buffered(1))


def kernel(x, meta, norm_g, w_in, b_gate, conv_a_w, w_proj_a, conv_b_w, conv_b_b, w_rg_a, b_rg_a, w_rg_i, b_rg_i, lru_param, w_proj_b, w_out, final_norm_g):
    bsz, seq, d = x.shape
    assert norm_g.shape[0] == 1, "single-layer trunk only"
    assert bsz % NB == 0 and seq % (2 * TT) == 0 and N_META % SLAB_T == 0
    assert w_in.shape[2] == N_SPLITS * d

    rows = [norm_g[0], *conv_a_w[0], *conv_b_w[0], conv_b_b[0], b_rg_a[0].reshape(d), b_rg_i[0].reshape(d),
            lru_param[0], b_gate[0, :d], b_gate[0, d:], final_norm_g]
    params = jnp.broadcast_to(jnp.stack(rows)[:, None, :], (N_PARAM_ROWS, SLAB, d)).astype(F32)
    w_in_b = _pack_rows(w_in[0])
    w_pa_b = _pack_rows(w_proj_a[0])
    w_pb_b = _pack_rows(w_proj_b[0])
    w_o_b = _pack_rows(w_out[0])
    w_ra_b = _pack_rows(w_rg_a[0])
    w_ri_b = _pack_rows(w_rg_i[0])
    xm = jnp.repeat(meta.astype(F32), NB, axis=0)

    mm = N_META * NB
    col = lambda k: pl.BlockSpec((d // 2, d), lambda i, k=k: (0, k))
    whole = lambda shape: pl.BlockSpec(shape, lambda i: (0,) * len(shape))
    st_a, st_b, st_h = pl.pallas_call(
        _meta_kernel,
        grid=(1,),
        in_specs=[whole((mm, d)), whole(params.shape), col(S_CA), col(S_HA), col(S_XB),
                  whole(w_ra_b.shape), whole(w_ri_b.shape)],
        out_specs=[whole((SLAB, d)), whole((3 * NB, d)), whole((NB, d))],
        out_shape=[jax.ShapeDtypeStruct((SLAB, d), F32), jax.ShapeDtypeStruct((3 * NB, d), F32),
                   jax.ShapeDtypeStruct((NB, d), F32)],
        scratch_shapes=[pltpu.VMEM((mm, d), BF16)] + [pltpu.VMEM((mm, d), F32)] * 4
                       + [pltpu.VMEM((mm, d), BF16)] + [pltpu.VMEM((mm, d), F32)] * 2
                       + [pltpu.VMEM((3 * NB, d), F32), pltpu.VMEM((NB, d), F32)],
        name="meta_state",
    )(xm, params, w_in_b, w_in_b, w_in_b, w_ra_b, w_ri_b)

    m = TT * NB
    act_f32 = pltpu.VMEM((m, d), F32)
    act_bf16 = pltpu.VMEM((m, d), BF16)
    return pl.pallas_call(
        _main_kernel,
        grid=(bsz // NB, seq // TT),
        in_specs=[pl.BlockSpec(memory_space=pl.ANY),
                  _const_spec((SLAB, d)), _const_spec((3 * NB, d)), _const_spec((NB, d)),
                  _const_spec(params.shape), _const_spec(w_in_b.shape), _const_spec(w_pa_b.shape),
                  _const_spec(w_pb_b.shape), _const_spec(w_o_b.shape), _const_spec(w_ra_b.shape),
                  _const_spec(w_ri_b.shape)],
        out_specs=pl.BlockSpec(memory_space=pl.ANY),
        out_shape=jax.ShapeDtypeStruct((bsz, seq, d), x.dtype),
        scratch_shapes=[
            pltpu.VMEM((2, TT, NB, d), F32),
            pltpu.VMEM((2, TT, NB, d), F32),
            pltpu.SemaphoreType.DMA((2,)),
            pltpu.SemaphoreType.DMA((2,)),
            pltpu.VMEM((SLAB, d), F32),
            pltpu.VMEM((3 * NB, d), F32),
            pltpu.VMEM((NB, d), F32),
            act_bf16,
            pltpu.VMEM((N_SPLITS, m, d), F32),
            act_bf16,
            act_f32, act_bf16,
            act_f32, act_f32,
            act_bf16,
            act_f32, act_f32,
            act_bf16,
            act_f32,
        ],
        compiler_params=pltpu.CompilerParams(
            dimension_semantics=("arbitrary", "arbitrary"),
            vmem_limit_bytes=V7X_VMEM_BYTES * 7 // 8,
        ),
        name="mixer_block",
    )(x, st_a, st_b, st_h, params, w_in_b, w_pa_b, w_pb_b, w_o_b, w_ra_b, w_ri_b)
```

```python
import jax
import jax.numpy as jnp
from jax import lax
from jax.experimental import pallas as pl
from jax.experimental.pallas import tpu as pltpu

N_META = 16
N_LRU_HEADS = 4
C_LRU = 8.0
RMS_EPS = 1e-6
SQRT_GUARD = 1e-30

V7X_F32_SUBLANES = 8
V7X_VMEM_BYTES = 64 * 1024 * 1024

NB = V7X_F32_SUBLANES
SLAB_T = 2
SLAB = SLAB_T * NB
TT = 32
N_XBUF = 4
N_OBUF = 2
N_SPLITS = 8
PACK_BLOCK_ROWS = 256
PACK_BLOCK_COLS = 2048

P_NORM_G, P_CONV_A, P_CONV_B, P_CONV_B_BIAS = 0, 1, 4, 8
P_B_RG_A, P_B_RG_I, P_LAM, P_BGATE_A, P_BGATE_B, P_FINAL_G = 9, 10, 11, 12, 13, 14
N_PARAM_ROWS = 15

S_BA, S_CA, S_HA, S_ZA, S_XB, S_ZB, S_GA, S_GB = range(N_SPLITS)

F32 = jnp.float32
BF16 = jnp.bfloat16


def _sigmoid(v):
    return 1.0 / (1.0 + jnp.exp(-v))


def _rms_norm(v, g):
    ms = jnp.mean(v * v, axis=-1, keepdims=True)
    return v * lax.rsqrt(ms + RMS_EPS) * g


def _slabs(n_rows):
    return [slice(j * SLAB, (j + 1) * SLAB) for j in range(n_rows // SLAB)]


def _dot(a, w_words):
    return jnp.dot(a, pltpu.bitcast(w_words, BF16), preferred_element_type=F32)


def _conv_b_phase(xb_ref, hist_b, p_ref, xc_buf, xcb_buf):
    cb = [p_ref[P_CONV_B + k] for k in range(4)]
    bias = p_ref[P_CONV_B_BIAS]
    p2h = hist_b[:NB, :]
    p1 = hist_b[NB:, :]
    for sl in _slabs(xb_ref.shape[0]):
        xb = xb_ref[sl, :]
        tap0 = jnp.concatenate([p2h, p1[:NB]], axis=0)
        tap2 = jnp.concatenate([p1[NB:], xb[:NB]], axis=0)
        xc = cb[0] * tap0 + cb[1] * p1 + cb[2] * tap2 + cb[3] * xb + bias
        xc_buf[sl, :] = xc
        xcb_buf[sl, :] = xc.astype(BF16)
        p2h = p1[NB:]
        p1 = xb
    hist_b[:NB, :] = p2h
    hist_b[NB:, :] = p1


def _gate_dots(xcb_buf, w_ra, w_ri, r_buf, i_buf):
    hd = xcb_buf.shape[1] // N_LRU_HEADS
    for h in range(N_LRU_HEADS):
        cs = slice(h * hd, (h + 1) * hd)
        r_buf[:, cs] = _dot(xcb_buf[:, cs], w_ra[h])
        i_buf[:, cs] = _dot(xcb_buf[:, cs], w_ri[h])


def _lru_phase(r_buf, i_buf, xc_buf, p_ref, hstate, first, zb_ref=None, yb_in=None):
    b_ra = p_ref[P_B_RG_A]
    b_ri = p_ref[P_B_RG_I]
    z = -p_ref[P_LAM]
    softplus = jnp.maximum(z, 0.0) + jnp.log1p(jnp.exp(-jnp.abs(z)))
    cs = -C_LRU * softplus
    h = hstate[...]
    for j, sl in enumerate(_slabs(r_buf.shape[0])):
        r = _sigmoid(r_buf[sl, :] + b_ra)
        ig = _sigmoid(i_buf[sl, :] + b_ri)
        log_a = cs * r
        a = jnp.exp(log_a)
        y = 1.0 - a * a
        mult = y * lax.rsqrt(jnp.maximum(y, SQRT_GUARD))
        if first and j == 0:
            mult = jnp.concatenate([jnp.ones((NB, mult.shape[1]), F32), mult[NB:]], axis=0)
        u = xc_buf[sl, :] * ig * mult
        h0 = a[:NB] * h + u[:NB]
        h = a[NB:] * h0 + u[NB:]
        if yb_in is not None:
            zb = zb_ref[sl, :]
            hs = jnp.concatenate([h0, h], axis=0)
            yb_in[sl, :] = (hs * (zb * _sigmoid(zb))).astype(BF16)
    hstate[...] = h


def _meta_kernel(xm_ref, p_ref, w_ca, w_ha, w_xb, w_ra, w_ri,
                 sta_ref, stb_ref, sth_ref,
                 xn_buf, pc_buf, ph_buf, px_buf, xc_buf, xcb_buf, r_buf, i_buf, hist_b, hstate):
    m = xm_ref.shape[0]
    g_norm = p_ref[P_NORM_G]
    for sl in _slabs(m):
        xn_buf[sl, :] = _rms_norm(xm_ref[sl, :], g_norm).astype(BF16)
    pc_buf[...] = _dot(xn_buf[...], w_ca[...])
    ph_buf[...] = _dot(xn_buf[...], w_ha[...])
    px_buf[...] = _dot(xn_buf[...], w_xb[...])
    last = slice(m - SLAB, m)
    sta_ref[...] = pc_buf[last, :] * ph_buf[last, :]
    hist_b[...] = jnp.zeros_like(hist_b)
    hstate[...] = jnp.zeros_like(hstate)
    _conv_b_phase(px_buf, hist_b, p_ref, xc_buf, xcb_buf)
    _gate_dots(xcb_buf, w_ra, w_ri, r_buf, i_buf)
    _lru_phase(r_buf, i_buf, xc_buf, p_ref, hstate, first=True)
    stb_ref[...] = hist_b[...]
    sth_ref[...] = hstate[...]


def _main_kernel(x_hbm, sta_ref, stb_ref, sth_ref, p_ref, w_in, w_pa, w_pb, w_o, w_ra, w_ri,
                 o_hbm,
                 xbuf, obuf, sem_in, sem_out, hist_a, hist_b, hstate,
                 xn_buf, ya_buf, yb_buf, ga_buf, gb_buf,
                 proj, ya_in, xc_buf, xcb_buf, r_buf, i_buf, yb_in, m_buf, o_buf):
    s = pl.program_id(0)
    n = pl.num_programs(0) - 1
    nc = x_hbm.shape[1] // TT
    m = TT * NB
    d = xn_buf.shape[2]
    slabs = _slabs(m)

    def in_copies(j):
        gg, cc, slot = lax.div(j, nc), lax.rem(j, nc), lax.rem(j, N_XBUF)
        return [pltpu.make_async_copy(x_hbm.at[gg * NB + b, pl.ds(cc * TT, TT), :],
                                      xbuf.at[slot, :, b, :], sem_in.at[slot]) for b in range(NB)]

    def out_copies(j):
        gg, cc, slot = lax.div(j, nc), lax.rem(j, nc), lax.rem(j, N_OBUF)
        return [pltpu.make_async_copy(obuf.at[slot, :, b, :],
                                      o_hbm.at[gg * NB + b, pl.ds(cc * TT, TT), :], sem_out.at[slot]) for b in range(NB)]

    def x_slab(x_ref, j):
        return jnp.concatenate([x_ref[j * SLAB_T + t] for t in range(SLAB_T)], axis=0)

    def pre_norm(x_ref, xn_ref):
        g_norm = p_ref[P_NORM_G]
        for j, sl in enumerate(slabs):
            xn_ref[sl, :] = _rms_norm(x_slab(x_ref, j), g_norm).astype(BF16)

    @pl.when(s == 0)
    def _():
        for j in range(2):
            for cp in in_copies(j):
                cp.start()
        xbuf[N_XBUF - 1] = jnp.zeros(xbuf.shape[1:], F32)
        for ref in (ya_buf, yb_buf, ga_buf, gb_buf):
            ref[...] = jnp.zeros_like(ref)
        for cp in in_copies(0):
            cp.wait()
        pre_norm(xbuf.at[0], xn_buf.at[0])

    @pl.when(s + 2 < n)
    def _():
        for cp in in_copies(s + 2):
            cp.start()

    @pl.when(s + 1 < n)
    def _():
        for cp in in_copies(s + 1):
            cp.wait()

    @pl.when(s >= 3)
    def _():
        for cp in out_copies(s - 3):
            cp.wait()

    @pl.when(lax.rem(s, nc) == 0)
    def _():
        hist_a[...] = sta_ref[...]
        hist_b[...] = stb_ref[...]
        hstate[...] = sth_ref[...]

    x_prev = xbuf.at[lax.rem(s + N_XBUF - 1, N_XBUF)]
    x_next = xbuf.at[lax.rem(s + 1, N_XBUF)]
    o_prev = obuf.at[lax.rem(s + N_OBUF - 1, N_OBUF)]
    xn_cur = xn_buf.at[lax.rem(s, 2)]
    xn_next = xn_buf.at[lax.rem(s + 1, 2)]

    def in_proj(k):
        proj[k] = _dot(xn_cur[...], w_in[:, k * d:(k + 1) * d])

    for sl in slabs:
        m_buf[sl, :] = (ga_buf[sl, :] * ya_buf[sl, :] + gb_buf[sl, :] * yb_buf[sl, :]).astype(BF16)

    in_proj(S_XB)
    in_proj(S_ZB)
    o_buf[...] = _dot(m_buf[...], w_o[...])

    _conv_b_phase(proj.at[S_XB], hist_b, p_ref, xc_buf, xcb_buf)
    in_proj(S_CA)
    in_proj(S_HA)
    _gate_dots(xcb_buf, w_ra, w_ri, r_buf, i_buf)

    g_fin = p_ref[P_FINAL_G]
    for j, sl in enumerate(slabs):
        out = _rms_norm(x_slab(x_prev, j) + o_buf[sl, :], g_fin)
        for t in range(SLAB_T):
            o_prev[j * SLAB_T + t] = out[t * NB:(t + 1) * NB]

    in_proj(S_BA)
    in_proj(S_ZA)

    _lru_phase(r_buf, i_buf, xc_buf, p_ref, hstate, first=False, zb_ref=proj.at[S_ZB], yb_in=yb_in)
    in_proj(S_GA)
    in_proj(S_GB)

    ca = [p_ref[P_CONV_A + k] for k in range(3)]
    prev = hist_a[...]
    for sl in slabs:
        ch = proj[S_CA, sl, :] * proj[S_HA, sl, :]
        tap1 = jnp.concatenate([prev[NB:], ch[:NB]], axis=0)
        conv = ca[0] * prev + ca[1] * tap1 + ca[2] * ch
        za = proj[S_ZA, sl, :]
        ya_in[sl, :] = (proj[S_BA, sl, :] * conv * (za * _sigmoid(za))).astype(BF16)
        prev = ch
    hist_a[...] = prev
    yb_buf[...] = _dot(yb_in[...], w_pb[...])

    bga = p_ref[P_BGATE_A]
    bgb = p_ref[P_BGATE_B]
    for sl in slabs:
        ga_buf[sl, :] = _sigmoid(proj[S_GA, sl, :] + bga)
        gb_buf[sl, :] = _sigmoid(proj[S_GB, sl, :] + bgb)
    ya_buf[...] = _dot(ya_in[...], w_pa[...])

    pre_norm(x_next, xn_next)

    @pl.when(s >= 1)
    def _():
        for cp in out_copies(s - 1):
            cp.start()

    @pl.when(s == n)
    def _():
        for j in (n - 2, n - 1):
            for cp in out_copies(j):
                cp.wait()


def _pack_kernel(*refs):
    n = len(refs) // 2
    for w_ref, o_ref in zip(refs[:n], refs[n:]):
        o_ref[...] = pltpu.bitcast(w_ref[...].astype(BF16), jnp.uint32)


def _pack_rows(*ws):
    k, n = ws[0].shape
    assert all(w.shape == (k, n) for w in ws)
    bk = min(k, PACK_BLOCK_ROWS)
    bn = min(n, PACK_BLOCK_COLS)
    assert k % bk == 0 and n % bn == 0
    return pl.pallas_call(
        _pack_kernel,
        grid=(k // bk, n // bn),
        in_specs=[pl.BlockSpec((bk, bn), lambda i, j: (i, j))] * len(ws),
        out_specs=[pl.BlockSpec((bk // 2, bn), lambda i, j: (i, j))] * len(ws),
        out_shape=[jax.ShapeDtypeStruct((k // 2, n), jnp.uint32)] * len(ws),
        name="pack_weights",
    )(*ws)


def _const_spec(shape):
    nd = len(shape)
    return pl.BlockSpec(shape, lambda *_: (0,) * nd, pipeline_mode=pl.Buffered(1))


def kernel(x, meta, norm_g, w_in, b_gate, conv_a_w, w_proj_a, conv_b_w, conv_b_b, w_rg_a, b_rg_a, w_rg_i, b_rg_i, lru_param, w_proj_b, w_out, final_norm_g):
    bsz, seq, d = x.shape
    assert norm_g.shape[0] == 1, "single-layer trunk only"
    assert bsz % NB == 0 and seq % TT == 0 and N_META % SLAB_T == 0
    assert w_in.shape[2] == N_SPLITS * d

    rows = [norm_g[0], *conv_a_w[0], *conv_b_w[0], conv_b_b[0], b_rg_a[0].reshape(d), b_rg_i[0].reshape(d),
            lru_param[0], b_gate[0, :d], b_gate[0, d:], final_norm_g]
    params = jnp.broadcast_to(jnp.stack(rows)[:, None, :], (N_PARAM_ROWS, SLAB, d)).astype(F32)
    hd = d // N_LRU_HEADS
    (w_in_b,) = _pack_rows(w_in[0])
    w_pa_b, w_pb_b, w_o_b = _pack_rows(w_proj_a[0], w_proj_b[0], w_out[0])
    w_ra_b, w_ri_b = (w.reshape(N_LRU_HEADS, hd // 2, hd)
                      for w in _pack_rows(w_rg_a[0].reshape(d, hd), w_rg_i[0].reshape(d, hd)))
    xm = jnp.repeat(meta.astype(F32), NB, axis=0)

    mm = N_META * NB
    col = lambda k: pl.BlockSpec((d // 2, d), lambda i, k=k: (0, k))
    whole = lambda shape: pl.BlockSpec(shape, lambda i: (0,) * len(shape))
    st_a, st_b, st_h = pl.pallas_call(
        _meta_kernel,
        grid=(1,),
        in_specs=[whole((mm, d)), whole(params.shape), col(S_CA), col(S_HA), col(S_XB),
                  whole(w_ra_b.shape), whole(w_ri_b.shape)],
        out_specs=[whole((SLAB, d)), whole((3 * NB, d)), whole((NB, d))],
        out_shape=[jax.ShapeDtypeStruct((SLAB, d), F32), jax.ShapeDtypeStruct((3 * NB, d), F32),
                   jax.ShapeDtypeStruct((NB, d), F32)],
        scratch_shapes=[pltpu.VMEM((mm, d), BF16)] + [pltpu.VMEM((mm, d), F32)] * 4
                       + [pltpu.VMEM((mm, d), BF16)] + [pltpu.VMEM((mm, d), F32)] * 2
                       + [pltpu.VMEM((3 * NB, d), F32), pltpu.VMEM((NB, d), F32)],
        name="meta_state",
    )(xm, params, w_in_b, w_in_b, w_in_b, w_ra_b, w_ri_b)

    m = TT * NB
    n_chunks = (bsz // NB) * (seq // TT)
    assert n_chunks >= N_XBUF
    act_f32 = pltpu.VMEM((m, d), F32)
    act_bf16 = pltpu.VMEM((m, d), BF16)
    return pl.pallas_call(
        _main_kernel,
        grid=(n_chunks + 1,),
        in_specs=[pl.BlockSpec(memory_space=pl.ANY),
                  _const_spec((SLAB, d)), _const_spec((3 * NB, d)), _const_spec((NB, d)),
                  _const_spec(params.shape), _const_spec(w_in_b.shape), _const_spec(w_pa_b.shape),
                  _const_spec(w_pb_b.shape), _const_spec(w_o_b.shape), _const_spec(w_ra_b.shape),
                  _const_spec(w_ri_b.shape)],
        out_specs=pl.BlockSpec(memory_space=pl.ANY),
        out_shape=jax.ShapeDtypeStruct((bsz, seq, d), x.dtype),
        scratch_shapes=[
            pltpu.VMEM((N_XBUF, TT, NB, d), F32),
            pltpu.VMEM((N_OBUF, TT, NB, d), F32),
            pltpu.SemaphoreType.DMA((N_XBUF,)),
            pltpu.SemaphoreType.DMA((N_OBUF,)),
            pltpu.VMEM((SLAB, d), F32),
            pltpu.VMEM((3 * NB, d), F32),
            pltpu.VMEM((NB, d), F32),
            pltpu.VMEM((2, m, d), BF16),
            act_f32, act_f32, act_f32, act_f32,
            pltpu.VMEM((N_SPLITS, m, d), F32),
            act_bf16,
            act_f32, act_bf16,
            act_f32, act_f32,
            act_bf16,
            act_bf16,
            act_f32,
        ],
        compiler_params=pltpu.CompilerParams(
            dimension_semantics=("arbitrary",),
            vmem_limit_bytes=V7X_VMEM_BYTES * 7 // 8,
        ),
        name="mixer_block",
    )(x, st_a, st_b, st_h, params, w_in_b, w_pa_b, w_pb_b, w_o_b, w_ra_b, w_ri_b)
```

```python
import jax
import jax.numpy as jnp
from jax import lax
from jax.experimental import pallas as pl
from jax.experimental.pallas import tpu as pltpu

N_META = 16
N_LRU_HEADS = 4
C_LRU = 8.0
RMS_EPS = 1e-6
SQRT_GUARD = 1e-30

V7X_F32_SUBLANES = 8
V7X_VMEM_BYTES = 64 * 1024 * 1024

NB = V7X_F32_SUBLANES
SLAB_T = 2
SLAB = SLAB_T * NB
TT = 32
N_XBUF = 4
N_OBUF = 2
N_SPLITS = 8
D_MODEL = 1024
CB = 512
N_COL_BLOCKS = D_MODEL // CB
PACK_BLOCK_ROWS = 256
PACK_BLOCK_COLS = 2048

P_NORM_G, P_CONV_A, P_CONV_B, P_CONV_B_BIAS = 0, 1, 4, 8
P_B_RG_A, P_B_RG_I, P_LAM, P_BGATE_A, P_BGATE_B, P_FINAL_G = 9, 10, 11, 12, 13, 14
N_PARAM_ROWS = 15

S_BA, S_CA, S_HA, S_ZA, S_XB, S_ZB, S_GA, S_GB = range(N_SPLITS)

F32 = jnp.float32
BF16 = jnp.bfloat16


def _sigmoid(v):
    return 1.0 / (1.0 + jnp.exp(-v))


def _rms_norm(v, g):
    ms = jnp.mean(v * v, axis=-1, keepdims=True)
    return v * lax.rsqrt(ms + RMS_EPS) * g


def _slabs(n_rows):
    return [slice(j * SLAB, (j + 1) * SLAB) for j in range(n_rows // SLAB)]


def _dot(a, w_words):
    return jnp.dot(a, pltpu.bitcast(w_words, BF16), preferred_element_type=F32)


def _cols(cb):
    return slice(cb * CB, (cb + 1) * CB)


def _in_proj_cols(xn, w_in, k, cb):
    d = xn.shape[1]
    return _dot(xn, w_in[:, k * d + cb * CB:k * d + (cb + 1) * CB])


def _conv_b_block(px, hist_b, p_ref, cols):
    cw = [p_ref[P_CONV_B + k, :, cols] for k in range(4)]
    bias = p_ref[P_CONV_B_BIAS, :, cols]
    p2h = hist_b[:NB, cols]
    p1 = hist_b[NB:, cols]
    xcs = []
    for sl in _slabs(px.shape[0]):
        xb = px[sl]
        tap0 = jnp.concatenate([p2h, p1[:NB]], axis=0)
        tap2 = jnp.concatenate([p1[NB:], xb[:NB]], axis=0)
        xcs.append(cw[0] * tap0 + cw[1] * p1 + cw[2] * tap2 + cw[3] * xb + bias)
        p2h = p1[NB:]
        p1 = xb
    hist_b[:NB, cols] = p2h
    hist_b[NB:, cols] = p1
    return xcs


def _gate_dots(xcs, w_ra, w_ri, cb):
    xcb = jnp.concatenate([xc.astype(BF16) for xc in xcs], axis=0)
    hd = w_ra.shape[2]
    heads = range(cb * (CB // hd), (cb + 1) * (CB // hd))
    parts = [xcb[:, k * hd:(k + 1) * hd] for k in range(CB // hd)]
    r = jnp.concatenate([_dot(p, w_ra[h]) for p, h in zip(parts, heads)], axis=1)
    i = jnp.concatenate([_dot(p, w_ri[h]) for p, h in zip(parts, heads)], axis=1)
    return r, i


def _lru_block(r, i, xcs, p_ref, hstate, cols, first, pzb=None, yb_in=None):
    b_ra = p_ref[P_B_RG_A, :, cols]
    b_ri = p_ref[P_B_RG_I, :, cols]
    z = -p_ref[P_LAM, :, cols]
    softplus = jnp.maximum(z, 0.0) + jnp.log1p(jnp.exp(-jnp.abs(z)))
    cs = -C_LRU * softplus
    h = hstate[:, cols]
    for j, sl in enumerate(_slabs(r.shape[0])):
        rg = _sigmoid(r[sl] + b_ra)
        ig = _sigmoid(i[sl] + b_ri)
        log_a = cs * rg
        a = jnp.exp(log_a)
        y = 1.0 - a * a
        mult = y * lax.rsqrt(jnp.maximum(y, SQRT_GUARD))
        if first and j == 0:
            mult = jnp.concatenate([jnp.ones((NB, mult.shape[1]), F32), mult[NB:]], axis=0)
        u = xcs[j] * ig * mult
        h0 = a[:NB] * h + u[:NB]
        h = a[NB:] * h0 + u[NB:]
        if yb_in is not None:
            zb = pzb[sl]
            hs = jnp.concatenate([h0, h], axis=0)
            yb_in[sl, cols] = (hs * (zb * _sigmoid(zb))).astype(BF16)
    hstate[:, cols] = h


def _meta_kernel(xm_ref, p_ref, w_ca, w_ha, w_xb, w_ra, w_ri,
                 sta_ref, stb_ref, sth_ref, xn_buf):
    m = xm_ref.shape[0]
    g_norm = p_ref[P_NORM_G]
    for sl in _slabs(m):
        xn_buf[sl, :] = _rms_norm(xm_ref[sl, :], g_norm).astype(BF16)
    xn = xn_buf[...]
    last = slice(m - SLAB, m)
    stb_ref[...] = jnp.zeros_like(stb_ref)
    sth_ref[...] = jnp.zeros_like(sth_ref)
    for cb in range(N_COL_BLOCKS):
        cols = _cols(cb)
        sta_ref[:, cols] = _dot(xn, w_ca[:, cols])[last] * _dot(xn, w_ha[:, cols])[last]
        xcs = _conv_b_block(_dot(xn, w_xb[:, cols]), stb_ref, p_ref, cols)
        r, i = _gate_dots(xcs, w_ra, w_ri, cb)
        _lru_block(r, i, xcs, p_ref, sth_ref, cols, first=True)


def _main_kernel(x_hbm, *refs):
    n = (x_hbm.shape[0] // NB) * (x_hbm.shape[1] // TT)

    def step(s, carry):
        _main_step(s, n, x_hbm, *refs)
        return carry

    lax.fori_loop(0, n + 1, step, 0)


def _main_step(s, n, x_hbm, sta_ref, stb_ref, sth_ref, p_ref, w_in, w_pa, w_pb, w_o, w_ra, w_ri,
               o_hbm,
               xbuf, obuf, sem_in, sem_out, hist_a, hist_b, hstate,
               xn_buf, ya_buf, yb_buf, ga_buf, gb_buf, ya_in, yb_in, m_buf):
    nc = x_hbm.shape[1] // TT
    m = TT * NB
    slabs = _slabs(m)

    def in_copies(j):
        gg, cc, slot = lax.div(j, nc), lax.rem(j, nc), lax.rem(j, N_XBUF)
        return [pltpu.make_async_copy(x_hbm.at[gg * NB + b, pl.ds(cc * TT, TT), :],
                                      xbuf.at[slot, :, b, :], sem_in.at[slot]) for b in range(NB)]

    def out_copies(j):
        gg, cc, slot = lax.div(j, nc), lax.rem(j, nc), lax.rem(j, N_OBUF)
        return [pltpu.make_async_copy(obuf.at[slot, :, b, :],
                                      o_hbm.at[gg * NB + b, pl.ds(cc * TT, TT), :], sem_out.at[slot]) for b in range(NB)]

    def x_slab(x_ref, j):
        return jnp.concatenate([x_ref[j * SLAB_T + t] for t in range(SLAB_T)], axis=0)

    def pre_norm(x_ref, xn_ref):
        g_norm = p_ref[P_NORM_G]
        for j, sl in enumerate(slabs):
            xn_ref[sl, :] = _rms_norm(x_slab(x_ref, j), g_norm).astype(BF16)

    @pl.when(s == 0)
    def _():
        for j in range(2):
            for cp in in_copies(j):
                cp.start()
        xbuf[N_XBUF - 1] = jnp.zeros(xbuf.shape[1:], F32)
        for ref in (ya_buf, yb_buf, ga_buf, gb_buf):
            ref[...] = jnp.zeros_like(ref)
        for cp in in_copies(0):
            cp.wait()
        pre_norm(xbuf.at[0], xn_buf.at[0])

    @pl.when(s + 2 < n)
    def _():
        for cp in in_copies(s + 2):
            cp.start()

    @pl.when(s + 1 < n)
    def _():
        for cp in in_copies(s + 1):
            cp.wait()

    @pl.when(s >= 3)
    def _():
        for cp in out_copies(s - 3):
            cp.wait()

    @pl.when(lax.rem(s, nc) == 0)
    def _():
        hist_a[...] = sta_ref[...]
        hist_b[...] = stb_ref[...]
        hstate[...] = sth_ref[...]

    x_prev = xbuf.at[lax.rem(s + N_XBUF - 1, N_XBUF)]
    x_next = xbuf.at[lax.rem(s + 1, N_XBUF)]
    o_prev = obuf.at[lax.rem(s + N_OBUF - 1, N_OBUF)]
    xn = xn_buf[lax.rem(s, 2)]
    xn_next = xn_buf.at[lax.rem(s + 1, 2)]

    def conv_b_and_gates(cb):
        xcs = _conv_b_block(_in_proj_cols(xn, w_in, S_XB, cb), hist_b, p_ref, _cols(cb))
        return xcs, _gate_dots(xcs, w_ra, w_ri, cb)

    def branch_a(cb):
        cols = _cols(cb)
        pc, ph, pb, pz = (_in_proj_cols(xn, w_in, k, cb) for k in (S_CA, S_HA, S_BA, S_ZA))
        cw = [p_ref[P_CONV_A + k, :, cols] for k in range(3)]
        prev = hist_a[:, cols]
        for sl in slabs:
            ch = pc[sl] * ph[sl]
            tap1 = jnp.concatenate([prev[NB:], ch[:NB]], axis=0)
            conv = cw[0] * prev + cw[1] * tap1 + cw[2] * ch
            za = pz[sl]
            ya_in[sl, cols] = (pb[sl] * conv * (za * _sigmoid(za))).astype(BF16)
            prev = ch
        hist_a[:, cols] = prev

    def merge_gates(cb):
        cols = _cols(cb)
        pga = _in_proj_cols(xn, w_in, S_GA, cb)
        pgb = _in_proj_cols(xn, w_in, S_GB, cb)
        bga = p_ref[P_BGATE_A, :, cols]
        bgb = p_ref[P_BGATE_B, :, cols]
        for sl in slabs:
            ga_buf[sl, cols] = _sigmoid(pga[sl] + bga)
            gb_buf[sl, cols] = _sigmoid(pgb[sl] + bgb)

    for sl in slabs:
        m_buf[sl, :] = (ga_buf[sl, :] * ya_buf[sl, :] + gb_buf[sl, :] * yb_buf[sl, :]).astype(BF16)
    xcs, (r, i) = conv_b_and_gates(0)
    o = _dot(m_buf[...], w_o[...])

    g_fin = p_ref[P_FINAL_G]
    for j, sl in enumerate(slabs):
        out = _rms_norm(x_slab(x_prev, j) + o[sl], g_fin)
        for t in range(SLAB_T):
            o_prev[j * SLAB_T + t] = out[t * NB:(t + 1) * NB]

    for cb in range(N_COL_BLOCKS):
        pzb = _in_proj_cols(xn, w_in, S_ZB, cb)
        _lru_block(r, i, xcs, p_ref, hstate, _cols(cb), first=False, pzb=pzb, yb_in=yb_in)
        if cb + 1 < N_COL_BLOCKS:
            xcs, (r, i) = conv_b_and_gates(cb + 1)
        branch_a(cb)
        merge_gates(cb)
    yb_buf[...] = _dot(yb_in[...], w_pb[...])
    ya_buf[...] = _dot(ya_in[...], w_pa[...])

    pre_norm(x_next, xn_next)

    @pl.when(s >= 1)
    def _():
        for cp in out_copies(s - 1):
            cp.start()

    @pl.when(s == n)
    def _():
        for j in (n - 2, n - 1):
            for cp in out_copies(j):
                cp.wait()


def _pack_kernel(*refs):
    n = len(refs) // 2
    for w_ref, o_ref in zip(refs[:n], refs[n:]):
        o_ref[...] = pltpu.bitcast(w_ref[...].astype(BF16), jnp.uint32)


def _pack_rows(*ws):
    k, n = ws[0].shape
    assert all(w.shape == (k, n) for w in ws)
    bk = min(k, PACK_BLOCK_ROWS)
    bn = min(n, PACK_BLOCK_COLS)
    assert k % bk == 0 and n % bn == 0
    return pl.pallas_call(
        _pack_kernel,
        grid=(k // bk, n // bn),
        in_specs=[pl.BlockSpec((bk, bn), lambda i, j: (i, j))] * len(ws),
        out_specs=[pl.BlockSpec((bk // 2, bn), lambda i, j: (i, j))] * len(ws),
        out_shape=[jax.ShapeDtypeStruct((k // 2, n), jnp.uint32)] * len(ws),
        name="pack_weights",
    )(*ws)


def _const_spec(shape):
    nd = len(shape)
    return pl.BlockSpec(shape, lambda *_: (0,) * nd, pipeline_mode=pl.Buffered(1))


def kernel(x, meta, norm_g, w_in, b_gate, conv_a_w, w_proj_a, conv_b_w, conv_b_b, w_rg_a, b_rg_a, w_rg_i, b_rg_i, lru_param, w_proj_b, w_out, final_norm_g):
    bsz, seq, d = x.shape
    assert norm_g.shape[0] == 1, "single-layer trunk only"
    assert bsz % NB == 0 and seq % TT == 0 and N_META % SLAB_T == 0
    assert w_in.shape[2] == N_SPLITS * d and d == D_MODEL and CB % (d // N_LRU_HEADS) == 0

    rows = [norm_g[0], *conv_a_w[0], *conv_b_w[0], conv_b_b[0], b_rg_a[0].reshape(d), b_rg_i[0].reshape(d),
            lru_param[0], b_gate[0, :d], b_gate[0, d:], final_norm_g]
    params = jnp.broadcast_to(jnp.stack(rows)[:, None, :], (N_PARAM_ROWS, SLAB, d)).astype(F32)
    hd = d // N_LRU_HEADS
    (w_in_b,) = _pack_rows(w_in[0])
    w_pa_b, w_pb_b, w_o_b = _pack_rows(w_proj_a[0], w_proj_b[0], w_out[0])
    w_ra_b, w_ri_b = (w.reshape(N_LRU_HEADS, hd // 2, hd)
                      for w in _pack_rows(w_rg_a[0].reshape(d, hd), w_rg_i[0].reshape(d, hd)))
    xm = jnp.repeat(meta.astype(F32), NB, axis=0)

    mm = N_META * NB
    col = lambda k: pl.BlockSpec((d // 2, d), lambda i, k=k: (0, k))
    whole = lambda shape: pl.BlockSpec(shape, lambda i: (0,) * len(shape))
    st_a, st_b, st_h = pl.pallas_call(
        _meta_kernel,
        grid=(1,),
        in_specs=[whole((mm, d)), whole(params.shape), col(S_CA), col(S_HA), col(S_XB),
                  whole(w_ra_b.shape), whole(w_ri_b.shape)],
        out_specs=[whole((SLAB, d)), whole((3 * NB, d)), whole((NB, d))],
        out_shape=[jax.ShapeDtypeStruct((SLAB, d), F32), jax.ShapeDtypeStruct((3 * NB, d), F32),
                   jax.ShapeDtypeStruct((NB, d), F32)],
        scratch_shapes=[pltpu.VMEM((mm, d), BF16)],
        name="meta_state",
    )(xm, params, w_in_b, w_in_b, w_in_b, w_ra_b, w_ri_b)

    m = TT * NB
    n_chunks = (bsz // NB) * (seq // TT)
    assert n_chunks >= N_XBUF
    act_f32 = pltpu.VMEM((m, d), F32)
    act_bf16 = pltpu.VMEM((m, d), BF16)
    return pl.pallas_call(
        _main_kernel,
        grid=(1,),
        in_specs=[pl.BlockSpec(memory_space=pl.ANY),
                  _const_spec((SLAB, d)), _const_spec((3 * NB, d)), _const_spec((NB, d)),
                  _const_spec(params.shape), _const_spec(w_in_b.shape), _const_spec(w_pa_b.shape),
                  _const_spec(w_pb_b.shape), _const_spec(w_o_b.shape), _const_spec(w_ra_b.shape),
                  _const_spec(w_ri_b.shape)],
        out_specs=pl.BlockSpec(memory_space=pl.ANY),
        out_shape=jax.ShapeDtypeStruct((bsz, seq, d), x.dtype),
        scratch_shapes=[
            pltpu.VMEM((N_XBUF, TT, NB, d), F32),
            pltpu.VMEM((N_OBUF, TT, NB, d), F32),
            pltpu.SemaphoreType.DMA((N_XBUF,)),
            pltpu.SemaphoreType.DMA((N_OBUF,)),
            pltpu.VMEM((SLAB, d), F32),
            pltpu.VMEM((3 * NB, d), F32),
            pltpu.VMEM((NB, d), F32),
            pltpu.VMEM((2, m, d), BF16),
            act_f32, act_f32, act_f32, act_f32,
            act_bf16, act_bf16, act_bf16,
        ],
        compiler_params=pltpu.CompilerParams(
            dimension_semantics=("arbitrary",),
            vmem_limit_bytes=V7X_VMEM_BYTES * 7 // 8,
        ),
        name="mixer_block",
    )(x, st_a, st_b, st_h, params, w_in_b, w_pa_b, w_pb_b, w_o_b, w_ra_b, w_ri_b)
```

```python
import jax
import jax.numpy as jnp
from jax import lax
from jax.experimental import pallas as pl
from jax.experimental.pallas import tpu as pltpu

N_META = 16
N_LRU_HEADS = 4
C_LRU = 8.0
RMS_EPS = 1e-6
SQRT_GUARD = 1e-30

V7X_F32_SUBLANES = 8
V7X_VMEM_BYTES = 64 * 1024 * 1024

NB = V7X_F32_SUBLANES
SLAB_T = 2
SLAB = SLAB_T * NB
TT = 32
TAIL_LAG = 2
N_XBUF = 5
N_OBUF = 2
N_XNBUF = 3
N_SPLITS = 8
D_MODEL = 1024
CB = 512
N_COL_BLOCKS = D_MODEL // CB
PACK_BLOCK_ROWS = 256
PACK_BLOCK_COLS = 2048

P_NORM_G, P_CONV_A, P_CONV_B, P_CONV_B_BIAS = 0, 1, 4, 8
P_B_RG_A, P_B_RG_I, P_LAM, P_BGATE_A, P_BGATE_B, P_FINAL_G = 9, 10, 11, 12, 13, 14
N_PARAM_ROWS = 15

S_BA, S_CA, S_HA, S_ZA, S_XB, S_ZB, S_GA, S_GB = range(N_SPLITS)

F32 = jnp.float32
BF16 = jnp.bfloat16


def _sigmoid(v):
    return 1.0 / (1.0 + jnp.exp(-v))


def _rms_norm(v, g):
    ms = jnp.mean(v * v, axis=-1, keepdims=True)
    return v * lax.rsqrt(ms + RMS_EPS) * g


def _slabs(n_rows):
    return [slice(j * SLAB, (j + 1) * SLAB) for j in range(n_rows // SLAB)]


def _dot(a, w_words):
    return jnp.dot(a, pltpu.bitcast(w_words, BF16), preferred_element_type=F32)


def _cols(cb):
    return slice(cb * CB, (cb + 1) * CB)


def _in_proj_cols(xn, w_in, k, cb):
    d = xn.shape[1]
    return _dot(xn, w_in[:, k * d + cb * CB:k * d + (cb + 1) * CB])


def _conv_b_block(px, hist_b, p_ref, cols, xc_buf, xcb_buf):
    cw = [p_ref[P_CONV_B + k, :, cols] for k in range(4)]
    bias = p_ref[P_CONV_B_BIAS, :, cols]
    p2h = hist_b[:NB, cols]
    p1 = hist_b[NB:, cols]
    for sl in _slabs(px.shape[0]):
        xb = px[sl]
        tap0 = jnp.concatenate([p2h, p1[:NB]], axis=0)
        tap2 = jnp.concatenate([p1[NB:], xb[:NB]], axis=0)
        xc = cw[0] * tap0 + cw[1] * p1 + cw[2] * tap2 + cw[3] * xb + bias
        xc_buf[sl, cols] = xc
        xcb_buf[sl, cols] = xc.astype(BF16)
        p2h = p1[NB:]
        p1 = xb
    hist_b[:NB, cols] = p2h
    hist_b[NB:, cols] = p1


def _gate_dots(xcb_buf, w_ra, w_ri, cb):
    hd = w_ra.shape[2]
    heads = range(cb * (CB // hd), (cb + 1) * (CB // hd))
    parts = [xcb_buf[:, h * hd:(h + 1) * hd] for h in heads]
    ri = [_dot(p, jnp.concatenate([w_ra[h], w_ri[h]], axis=1)) for p, h in zip(parts, heads)]
    r = jnp.concatenate([x[:, :hd] for x in ri], axis=1)
    i = jnp.concatenate([x[:, hd:] for x in ri], axis=1)
    return r, i


def _lru_block(r, i, xc_buf, p_ref, hstate, cols, first, pzb=None, yb_in=None):
    b_ra = p_ref[P_B_RG_A, :, cols]
    b_ri = p_ref[P_B_RG_I, :, cols]
    z = -p_ref[P_LAM, :, cols]
    softplus = jnp.maximum(z, 0.0) + jnp.log1p(jnp.exp(-jnp.abs(z)))
    cs = -C_LRU * softplus
    h = hstate[:, cols]
    for j, sl in enumerate(_slabs(r.shape[0])):
        rg = _sigmoid(r[sl] + b_ra)
        ig = _sigmoid(i[sl] + b_ri)
        log_a = cs * rg
        a = jnp.exp(log_a)
        y = 1.0 - a * a
        mult = y * lax.rsqrt(jnp.maximum(y, SQRT_GUARD))
        if first and j == 0:
            mult = jnp.concatenate([jnp.ones((NB, mult.shape[1]), F32), mult[NB:]], axis=0)
        u = xc_buf[sl, cols] * ig * mult
        h0 = a[:NB] * h + u[:NB]
        h = a[NB:] * h0 + u[NB:]
        if yb_in is not None:
            zb = pzb[sl]
            hs = jnp.concatenate([h0, h], axis=0)
            yb_in[sl, cols] = (hs * (zb * _sigmoid(zb))).astype(BF16)
    hstate[:, cols] = h


def _meta_kernel(xm_ref, p_ref, w_ca, w_ha, w_xb, w_ra, w_ri,
                 sta_ref, stb_ref, sth_ref, xn_buf, xc_buf, xcb_buf):
    m = xm_ref.shape[0]
    g_norm = p_ref[P_NORM_G]
    for sl in _slabs(m):
        xn_buf[sl, :] = _rms_norm(xm_ref[sl, :], g_norm).astype(BF16)
    xn = xn_buf[...]
    last = slice(m - SLAB, m)
    stb_ref[...] = jnp.zeros_like(stb_ref)
    sth_ref[...] = jnp.zeros_like(sth_ref)
    for cb in range(N_COL_BLOCKS):
        cols = _cols(cb)
        sta_ref[:, cols] = _dot(xn, w_ca[:, cols])[last] * _dot(xn, w_ha[:, cols])[last]
        _conv_b_block(_dot(xn, w_xb[:, cols]), stb_ref, p_ref, cols, xc_buf, xcb_buf)
        r, i = _gate_dots(xcb_buf, w_ra, w_ri, cb)
        _lru_block(r, i, xc_buf, p_ref, sth_ref, cols, first=True)


def _main_kernel(x_hbm, *refs):
    n = (x_hbm.shape[0] // NB) * (x_hbm.shape[1] // TT)
    _main_step(pl.program_id(0), n, x_hbm, *refs)


def _main_step(s, n, x_hbm, sta_ref, stb_ref, sth_ref, p_ref, w_in, w_pa, w_pb, w_o, w_ra, w_ri,
               o_hbm,
               xbuf, obuf, sem_in, sem_out, hist_a, hist_b, hstate,
               xn_buf, xc_buf, xcb_buf, ya_buf, yb_buf, ga_buf, gb_buf, ya_in, yb_in, m_buf):
    nc = x_hbm.shape[1] // TT
    m = TT * NB
    slabs = _slabs(m)
    blocks = range(N_COL_BLOCKS)

    def in_copies(j):
        gg, cc, slot = lax.div(j, nc), lax.rem(j, nc), lax.rem(j, N_XBUF)
        return [pltpu.make_async_copy(x_hbm.at[gg * NB + b, pl.ds(cc * TT, TT), :],
                                      xbuf.at[slot, :, b, :], sem_in.at[slot]) for b in range(NB)]

    def out_copies(j):
        gg, cc, slot = lax.div(j, nc), lax.rem(j, nc), lax.rem(j, N_OBUF)
        return [pltpu.make_async_copy(obuf.at[slot, :, b, :],
                                      o_hbm.at[gg * NB + b, pl.ds(cc * TT, TT), :], sem_out.at[slot]) for b in range(NB)]

    def x_slab(x_ref, j):
        return jnp.concatenate([x_ref[j * SLAB_T + t] for t in range(SLAB_T)], axis=0)

    def pre_norm(x_ref, xn_ref):
        g_norm = p_ref[P_NORM_G]
        for j, sl in enumerate(slabs):
            xn_ref[sl, :] = _rms_norm(x_slab(x_ref, j), g_norm).astype(BF16)

    @pl.when(s == 0)
    def _():
        for j in range(2):
            for cp in in_copies(j):
                cp.start()
        for slot in range(N_XBUF - TAIL_LAG, N_XBUF):
            xbuf[slot] = jnp.zeros(xbuf.shape[1:], F32)
        xn_buf[N_XNBUF - 1] = jnp.zeros(xn_buf.shape[1:], BF16)
        for ref in (xc_buf, xcb_buf, ya_buf, yb_buf, ga_buf, gb_buf, hist_a, hstate):
            ref[...] = jnp.zeros_like(ref)
        for cp in in_copies(0):
            cp.wait()
        pre_norm(xbuf.at[0], xn_buf.at[0])

    @pl.when(s + 2 < n)
    def _():
        for cp in in_copies(s + 2):
            cp.start()

    @pl.when(s + 1 < n)
    def _():
        for cp in in_copies(s + 1):
            cp.wait()

    @pl.when(s >= TAIL_LAG + N_OBUF)
    def _():
        for cp in out_copies(s - TAIL_LAG - N_OBUF):
            cp.wait()

    @pl.when(lax.rem(s, nc) == 0)
    def _():
        hist_b[...] = stb_ref[...]

    @pl.when(lax.rem(s + nc - 1, nc) == 0)
    def _():
        hist_a[...] = sta_ref[...]
        hstate[...] = sth_ref[...]

    x_tail = xbuf.at[lax.rem(s + (N_XBUF - 1) * TAIL_LAG, N_XBUF)]
    x_next = xbuf.at[lax.rem(s + 1, N_XBUF)]
    o_tail = obuf.at[lax.rem(s + (N_OBUF - 1) * TAIL_LAG, N_OBUF)]
    xn_front = xn_buf[lax.rem(s, N_XNBUF)]
    xn = xn_buf[lax.rem(s + N_XNBUF - 1, N_XNBUF)]
    xn_next = xn_buf.at[lax.rem(s + 1, N_XNBUF)]

    def proj_dots(k):
        return [_in_proj_cols(xn, w_in, k, cb) for cb in blocks]

    def branch_a(cb, pc, ph, pb, pz):
        cols = _cols(cb)
        cw = [p_ref[P_CONV_A + k, :, cols] for k in range(3)]
        prev = hist_a[:, cols]
        for sl in slabs:
            ch = pc[sl] * ph[sl]
            tap1 = jnp.concatenate([prev[NB:], ch[:NB]], axis=0)
            conv = cw[0] * prev + cw[1] * tap1 + cw[2] * ch
            za = pz[sl]
            ya_in[sl, cols] = (pb[sl] * conv * (za * _sigmoid(za))).astype(BF16)
            prev = ch
        hist_a[:, cols] = prev

    def merge_gates(cb, pga, pgb):
        cols = _cols(cb)
        bga = p_ref[P_BGATE_A, :, cols]
        bgb = p_ref[P_BGATE_B, :, cols]
        for sl in slabs:
            ga_buf[sl, cols] = _sigmoid(pga[sl] + bga)
            gb_buf[sl, cols] = _sigmoid(pgb[sl] + bgb)

    for sl in slabs:
        m_buf[sl, :] = (ga_buf[sl, :] * ya_buf[sl, :] + gb_buf[sl, :] * yb_buf[sl, :]).astype(BF16)

    gates = [_gate_dots(xcb_buf, w_ra, w_ri, cb) for cb in blocks]
    pzb = proj_dots(S_ZB)
    o = _dot(m_buf[...], w_o[...])
    for cb in blocks:
        r, i = gates[cb]
        _lru_block(r, i, xc_buf, p_ref, hstate, _cols(cb), first=False, pzb=pzb[cb], yb_in=yb_in)

    for cb in blocks:
        _conv_b_block(_in_proj_cols(xn_front, w_in, S_XB, cb), hist_b, p_ref, _cols(cb), xc_buf, xcb_buf)

    g_fin = p_ref[P_FINAL_G]
    for j, sl in enumerate(slabs):
        out = _rms_norm(x_slab(x_tail, j) + o[sl], g_fin)
        for t in range(SLAB_T):
            o_tail[j * SLAB_T + t] = out[t * NB:(t + 1) * NB]

    for cb in blocks:
        branch_a(cb, *[_in_proj_cols(xn, w_in, k, cb) for k in (S_CA, S_HA, S_BA, S_ZA)])
    pga = proj_dots(S_GA)
    pgb = proj_dots(S_GB)
    for cb in blocks:
        merge_gates(cb, pga[cb], pgb[cb])
    yb_buf[...] = _dot(yb_in[...], w_pb[...])
    ya_buf[...] = _dot(ya_in[...], w_pa[...])

    pre_norm(x_next, xn_next)

    @pl.when(s >= TAIL_LAG)
    def _():
        for cp in out_copies(s - TAIL_LAG):
            cp.start()

    @pl.when(s == n + TAIL_LAG - 1)
    def _():
        for j in (n - 2, n - 1):
            for cp in out_copies(j):
                cp.wait()


def _pack_kernel(*refs):
    n = len(refs) // 2
    for w_ref, o_ref in zip(refs[:n], refs[n:]):
        o_ref[...] = pltpu.bitcast(w_ref[...].astype(BF16), jnp.uint32)


def _pack_rows(*ws):
    k, n = ws[0].shape
    assert all(w.shape == (k, n) for w in ws)
    bk = min(k, PACK_BLOCK_ROWS)
    bn = min(n, PACK_BLOCK_COLS)
    assert k % bk == 0 and n % bn == 0
    return pl.pallas_call(
        _pack_kernel,
        grid=(k // bk, n // bn),
        in_specs=[pl.BlockSpec((bk, bn), lambda i, j: (i, j))] * len(ws),
        out_specs=[pl.BlockSpec((bk // 2, bn), lambda i, j: (i, j))] * len(ws),
        out_shape=[jax.ShapeDtypeStruct((k // 2, n), jnp.uint32)] * len(ws),
        name="pack_weights",
    )(*ws)


def _const_spec(shape):
    nd = len(shape)
    return pl.BlockSpec(shape, lambda *_: (0,) * nd, pipeline_mode=pl.Buffered(1))


def kernel(x, meta, norm_g, w_in, b_gate, conv_a_w, w_proj_a, conv_b_w, conv_b_b, w_rg_a, b_rg_a, w_rg_i, b_rg_i, lru_param, w_proj_b, w_out, final_norm_g):
    bsz, seq, d = x.shape
    assert norm_g.shape[0] == 1, "single-layer trunk only"
    assert bsz % NB == 0 and seq % TT == 0 and N_META % SLAB_T == 0
    assert w_in.shape[2] == N_SPLITS * d and d == D_MODEL and CB % (d // N_LRU_HEADS) == 0

    rows = [norm_g[0], *conv_a_w[0], *conv_b_w[0], conv_b_b[0], b_rg_a[0].reshape(d), b_rg_i[0].reshape(d),
            lru_param[0], b_gate[0, :d], b_gate[0, d:], final_norm_g]
    params = jnp.broadcast_to(jnp.stack(rows)[:, None, :], (N_PARAM_ROWS, SLAB, d)).astype(F32)
    hd = d // N_LRU_HEADS
    (w_in_b,) = _pack_rows(w_in[0])
    w_pa_b, w_pb_b, w_o_b = _pack_rows(w_proj_a[0], w_proj_b[0], w_out[0])
    w_ra_b, w_ri_b = (w.reshape(N_LRU_HEADS, hd // 2, hd)
                      for w in _pack_rows(w_rg_a[0].reshape(d, hd), w_rg_i[0].reshape(d, hd)))
    xm = jnp.repeat(meta.astype(F32), NB, axis=0)

    mm = N_META * NB
    col = lambda k: pl.BlockSpec((d // 2, d), lambda i, k=k: (0, k))
    whole = lambda shape: pl.BlockSpec(shape, lambda i: (0,) * len(shape))
    st_a, st_b, st_h = pl.pallas_call(
        _meta_kernel,
        grid=(1,),
        in_specs=[whole((mm, d)), whole(params.shape), col(S_CA), col(S_HA), col(S_XB),
                  whole(w_ra_b.shape), whole(w_ri_b.shape)],
        out_specs=[whole((SLAB, d)), whole((3 * NB, d)), whole((NB, d))],
        out_shape=[jax.ShapeDtypeStruct((SLAB, d), F32), jax.ShapeDtypeStruct((3 * NB, d), F32),
                   jax.ShapeDtypeStruct((NB, d), F32)],
        scratch_shapes=[pltpu.VMEM((mm, d), BF16), pltpu.VMEM((mm, d), F32), pltpu.VMEM((mm, d), BF16)],
        name="meta_state",
    )(xm, params, w_in_b, w_in_b, w_in_b, w_ra_b, w_ri_b)

    m = TT * NB
    n_chunks = (bsz // NB) * (seq // TT)
    assert n_chunks >= N_XBUF
    act_f32 = pltpu.VMEM((m, d), F32)
    act_bf16 = pltpu.VMEM((m, d), BF16)
    return pl.pallas_call(
        _main_kernel,
        grid=(n_chunks + TAIL_LAG,),
        in_specs=[pl.BlockSpec(memory_space=pl.ANY),
                  _const_spec((SLAB, d)), _const_spec((3 * NB, d)), _const_spec((NB, d)),
                  _const_spec(params.shape), _const_spec(w_in_b.shape), _const_spec(w_pa_b.shape),
                  _const_spec(w_pb_b.shape), _const_spec(w_o_b.shape), _const_spec(w_ra_b.shape),
                  _const_spec(w_ri_b.shape)],
        out_specs=pl.BlockSpec(memory_space=pl.ANY),
        out_shape=jax.ShapeDtypeStruct((bsz, seq, d), x.dtype),
        scratch_shapes=[
            pltpu.VMEM((N_XBUF, TT, NB, d), F32),
            pltpu.VMEM((N_OBUF, TT, NB, d), F32),
            pltpu.SemaphoreType.DMA((N_XBUF,)),
            pltpu.SemaphoreType.DMA((N_OBUF,)),
            pltpu.VMEM((SLAB, d), F32),
            pltpu.VMEM((3 * NB, d), F32),
            pltpu.VMEM((NB, d), F32),
            pltpu.VMEM((N_XNBUF, m, d), BF16),
            act_f32, act_bf16,
            act_f32, act_f32, act_f32, act_f32,
            act_bf16, act_bf16, act_bf16,
        ],
        compiler_params=pltpu.CompilerParams(
            dimension_semantics=("arbitrary",),
            vmem_limit_bytes=V7X_VMEM_BYTES * 7 // 8,
        ),
        name="mixer_block",
    )(x, st_a, st_b, st_h, params, w_in_b, w_pa_b, w_pb_b, w_o_b, w_ra_b, w_ri_b)
```

```python
import jax
import jax.numpy as jnp
from jax import lax
from jax.experimental import pallas as pl
from jax.experimental.pallas import tpu as pltpu

N_META = 16
N_LRU_HEADS = 4
C_LRU = 8.0
RMS_EPS = 1e-6
LOG2_E = 1.4426950408889634
SQRT_GUARD = 1e-30

V7X_F32_SUBLANES = 8
V7X_VMEM_BYTES = 64 * 1024 * 1024

NB = V7X_F32_SUBLANES
SLAB_T = 2
SLAB = SLAB_T * NB
TT = 32
UNROLL = 1
TAIL_LAG = 2
N_XBUF = 2 * UNROLL + 3
N_OBUF = 2 * UNROLL
N_XNBUF = 3
N_SPLITS = 8
D_MODEL = 1024
CB = 512
N_COL_BLOCKS = D_MODEL // CB
PACK_BLOCK_ROWS = 256
PACK_BLOCK_COLS = 2048

P_NORM_G, P_CONV_A, P_CONV_B, P_CONV_B_BIAS = 0, 1, 4, 8
P_B_RG_A, P_B_RG_I, P_LAM, P_BGATE_A, P_BGATE_B, P_FINAL_G = 9, 10, 11, 12, 13, 14
N_PARAM_ROWS = 15

S_BA, S_CA, S_HA, S_ZA, S_XB, S_ZB, S_GA, S_GB = range(N_SPLITS)

F32 = jnp.float32
BF16 = jnp.bfloat16


def _sigmoid(v):
    return 1.0 / (1.0 + jnp.exp2(v * -LOG2_E))


def _rms_norm(v, g):
    ms = jnp.mean(v * v, axis=-1, keepdims=True)
    return v * lax.rsqrt(ms + RMS_EPS) * g


def _slabs(n_rows):
    return [slice(j * SLAB, (j + 1) * SLAB) for j in range(n_rows // SLAB)]


def _dot(a, w_words):
    return jnp.dot(a, pltpu.bitcast(w_words, BF16), preferred_element_type=F32)


def _cols(cb):
    return slice(cb * CB, (cb + 1) * CB)


def _in_proj_cols(xn, w_in, k, cb):
    d = xn.shape[1]
    return _dot(xn, w_in[:, k * d + cb * CB:k * d + (cb + 1) * CB])


def _conv_b_block(px, hist_b, p_ref, cols, xc_buf, xcb_buf):
    cw = [p_ref[P_CONV_B + k, :, cols] for k in range(4)]
    bias = p_ref[P_CONV_B_BIAS, :, cols]
    p2h = hist_b[:NB, cols]
    p1 = hist_b[NB:, cols]
    for sl in _slabs(px.shape[0]):
        xb = px[sl]
        tap0 = jnp.concatenate([p2h, p1[:NB]], axis=0)
        tap2 = jnp.concatenate([p1[NB:], xb[:NB]], axis=0)
        xc = cw[0] * tap0 + cw[1] * p1 + cw[2] * tap2 + cw[3] * xb + bias
        xc_buf[sl, cols] = xc
        xcb_buf[sl, cols] = xc.astype(BF16)
        p2h = p1[NB:]
        p1 = xb
    hist_b[:NB, cols] = p2h
    hist_b[NB:, cols] = p1


def _gate_dots(xcb_buf, w_ra, w_ri, cb):
    hd = w_ra.shape[2]
    heads = range(cb * (CB // hd), (cb + 1) * (CB // hd))
    parts = [xcb_buf[:, h * hd:(h + 1) * hd] for h in heads]
    ri = [_dot(p, jnp.concatenate([w_ra[h], w_ri[h]], axis=1)) for p, h in zip(parts, heads)]
    r = jnp.concatenate([x[:, :hd] for x in ri], axis=1)
    i = jnp.concatenate([x[:, hd:] for x in ri], axis=1)
    return r, i


def _lru_block(r, i, xc_buf, p_ref, hstate, cols, first, pzb=None, yb_in=None):
    b_ra = p_ref[P_B_RG_A, :, cols]
    b_ri = p_ref[P_B_RG_I, :, cols]
    z = -p_ref[P_LAM, :, cols]
    softplus = jnp.maximum(z, 0.0) + jnp.log1p(jnp.exp(-jnp.abs(z)))
    cs = (-C_LRU * LOG2_E) * softplus
    h = hstate[:, cols]
    for j, sl in enumerate(_slabs(r.shape[0])):
        rg = _sigmoid(r[sl] + b_ra)
        ig = _sigmoid(i[sl] + b_ri)
        a = jnp.exp2(cs * rg)
        y = 1.0 - a * a
        mult = y * lax.rsqrt(jnp.maximum(y, SQRT_GUARD))
        if first and j == 0:
            mult = jnp.concatenate([jnp.ones((NB, mult.shape[1]), F32), mult[NB:]], axis=0)
        u = xc_buf[sl, cols] * ig * mult
        h0 = a[:NB] * h + u[:NB]
        h = a[NB:] * h0 + u[NB:]
        if yb_in is not None:
            zb = pzb[sl]
            hs = jnp.concatenate([h0, h], axis=0)
            yb_in[sl, cols] = (hs * (zb * _sigmoid(zb))).astype(BF16)
    hstate[:, cols] = h


def _meta_kernel(xm_ref, p_ref, w_ca, w_ha, w_xb, w_ra, w_ri,
                 sta_ref, stb_ref, sth_ref, xn_buf, xc_buf, xcb_buf):
    m = xm_ref.shape[0]
    g_norm = p_ref[P_NORM_G]
    for sl in _slabs(m):
        xn_buf[sl, :] = _rms_norm(xm_ref[sl, :], g_norm).astype(BF16)
    xn = xn_buf[...]
    last = slice(m - SLAB, m)
    stb_ref[...] = jnp.zeros_like(stb_ref)
    sth_ref[...] = jnp.zeros_like(sth_ref)
    for cb in range(N_COL_BLOCKS):
        cols = _cols(cb)
        sta_ref[:, cols] = _dot(xn, w_ca[:, cols])[last] * _dot(xn, w_ha[:, cols])[last]
        _conv_b_block(_dot(xn, w_xb[:, cols]), stb_ref, p_ref, cols, xc_buf, xcb_buf)
        r, i = _gate_dots(xcb_buf, w_ra, w_ri, cb)
        _lru_block(r, i, xc_buf, p_ref, sth_ref, cols, first=True)


def _main_kernel(x_hbm, sta_ref, stb_ref, sth_ref, p_ref, w_in, w_pa, w_pb, w_o, w_ra, w_ri,
                 o_hbm,
                 xbuf, obuf, sem_in, sem_out, hist_a, hist_b, hstate,
                 xn_buf, xc_buf, xcb_buf, ya_buf, yb_buf, ga_buf, gb_buf, ya_in, yb_in, m_buf):
    step = pl.program_id(0)
    n = (x_hbm.shape[0] // NB) * (x_hbm.shape[1] // TT)
    nc = x_hbm.shape[1] // TT
    q0 = step * UNROLL

    def in_copies(j):
        gg, cc, slot = lax.div(j, nc), lax.rem(j, nc), lax.rem(j, N_XBUF)
        return [pltpu.make_async_copy(x_hbm.at[gg * NB + b, pl.ds(cc * TT, TT), :],
                                      xbuf.at[slot, :, b, :], sem_in.at[slot]) for b in range(NB)]

    def out_copies(j):
        gg, cc, slot = lax.div(j, nc), lax.rem(j, nc), lax.rem(j, N_OBUF)
        return [pltpu.make_async_copy(obuf.at[slot, :, b, :],
                                      o_hbm.at[gg * NB + b, pl.ds(cc * TT, TT), :], sem_out.at[slot]) for b in range(NB)]

    def guarded(cond, copies_of, j, start):
        @pl.when(cond)
        def _():
            for cp in copies_of(j):
                cp.start() if start else cp.wait()

    @pl.when(step == 0)
    def _():
        for j in range(UNROLL + 1):
            for cp in in_copies(j):
                cp.start()
        for slot in range(N_XBUF - TAIL_LAG, N_XBUF):
            xbuf[slot] = jnp.zeros(xbuf.shape[1:], F32)
        xn_buf[N_XNBUF - 1] = jnp.zeros(xn_buf.shape[1:], BF16)
        for ref in (xc_buf, xcb_buf, ya_buf, yb_buf, ga_buf, gb_buf, hist_a, hist_b, hstate):
            ref[...] = jnp.zeros_like(ref)
        for cp in in_copies(0):
            cp.wait()
        _pre_norm(xbuf.at[0], xn_buf.at[0], p_ref)

    for u in range(UNROLL):
        guarded(q0 + UNROLL + 1 + u < n, in_copies, q0 + UNROLL + 1 + u, start=True)
    for u in range(UNROLL):
        guarded(q0 + 1 + u < n, in_copies, q0 + 1 + u, start=False)
    for u in range(UNROLL):
        guarded(q0 + u >= TAIL_LAG + N_OBUF, out_copies, q0 + u - TAIL_LAG - N_OBUF, start=False)

    for u in range(UNROLL):
        _pipeline_position(q0 + u, nc, sta_ref, stb_ref, sth_ref, p_ref, w_in, w_pa, w_pb, w_o, w_ra, w_ri,
                           xbuf, obuf, hist_a, hist_b, hstate,
                           xn_buf, xc_buf, xcb_buf, ya_buf, yb_buf, ga_buf, gb_buf, ya_in, yb_in, m_buf)

    for u in range(UNROLL):
        guarded(q0 + u >= TAIL_LAG, out_copies, q0 + u - TAIL_LAG, start=True)

    @pl.when(step == pl.num_programs(0) - 1)
    def _():
        for j in range(n - N_OBUF, n):
            for cp in out_copies(j):
                cp.wait()


def _x_slab(x_ref, j):
    return jnp.concatenate([x_ref[j * SLAB_T + t] for t in range(SLAB_T)], axis=0)


def _pre_norm(x_ref, xn_ref, p_ref):
    g_norm = p_ref[P_NORM_G]
    for j, sl in enumerate(_slabs(TT * NB)):
        xn_ref[sl, :] = _rms_norm(_x_slab(x_ref, j), g_norm).astype(BF16)


def _pipeline_position(q, nc, sta_ref, stb_ref, sth_ref, p_ref, w_in, w_pa, w_pb, w_o, w_ra, w_ri,
                       xbuf, obuf, hist_a, hist_b, hstate,
                       xn_buf, xc_buf, xcb_buf, ya_buf, yb_buf, ga_buf, gb_buf, ya_in, yb_in, m_buf):
    slabs = _slabs(TT * NB)
    blocks = range(N_COL_BLOCKS)

    front_first = lax.rem(q, nc) == 0
    body_first = lax.rem(q + nc - 1, nc) == 0
    hist_b[...] = jnp.where(front_first, stb_ref[...], hist_b[...])
    hist_a[...] = jnp.where(body_first, sta_ref[...], hist_a[...])
    hstate[...] = jnp.where(body_first, sth_ref[...], hstate[...])

    x_tail = xbuf.at[lax.rem(q + (N_XBUF - 1) * TAIL_LAG, N_XBUF)]
    x_next = xbuf.at[lax.rem(q + 1, N_XBUF)]
    o_tail = obuf.at[lax.rem(q + (N_OBUF - 1) * TAIL_LAG, N_OBUF)]
    xn_front = xn_buf[lax.rem(q, N_XNBUF)]
    xn = xn_buf[lax.rem(q + N_XNBUF - 1, N_XNBUF)]
    xn_next = xn_buf.at[lax.rem(q + 1, N_XNBUF)]

    def proj_dots(k):
        return [_in_proj_cols(xn, w_in, k, cb) for cb in blocks]

    def branch_a(cb, pc, ph, pb, pz):
        cols = _cols(cb)
        cw = [p_ref[P_CONV_A + k, :, cols] for k in range(3)]
        prev = hist_a[:, cols]
        for sl in slabs:
            ch = pc[sl] * ph[sl]
            tap1 = jnp.concatenate([prev[NB:], ch[:NB]], axis=0)
            conv = cw[0] * prev + cw[1] * tap1 + cw[2] * ch
            za = pz[sl]
            ya_in[sl, cols] = (pb[sl] * conv * (za * _sigmoid(za))).astype(BF16)
            prev = ch
        hist_a[:, cols] = prev

    def merge_gates(cb, pga, pgb):
        cols = _cols(cb)
        bga = p_ref[P_BGATE_A, :, cols]
        bgb = p_ref[P_BGATE_B, :, cols]
        for sl in slabs:
            ga_buf[sl, cols] = _sigmoid(pga[sl] + bga)
            gb_buf[sl, cols] = _sigmoid(pgb[sl] + bgb)

    for sl in slabs:
        m_buf[sl, :] = (ga_buf[sl, :] * ya_buf[sl, :] + gb_buf[sl, :] * yb_buf[sl, :]).astype(BF16)

    gates = [_gate_dots(xcb_buf, w_ra, w_ri, cb) for cb in blocks]
    pzb = proj_dots(S_ZB)
    o = _dot(m_buf[...], w_o[...])
    for cb in blocks:
        r, i = gates[cb]
        _lru_block(r, i, xc_buf, p_ref, hstate, _cols(cb), first=False, pzb=pzb[cb], yb_in=yb_in)

    for cb in blocks:
        _conv_b_block(_in_proj_cols(xn_front, w_in, S_XB, cb), hist_b, p_ref, _cols(cb), xc_buf, xcb_buf)

    g_fin = p_ref[P_FINAL_G]
    for j, sl in enumerate(slabs):
        out = _rms_norm(_x_slab(x_tail, j) + o[sl], g_fin)
        for t in range(SLAB_T):
            o_tail[j * SLAB_T + t] = out[t * NB:(t + 1) * NB]

    for cb in blocks:
        branch_a(cb, *[_in_proj_cols(xn, w_in, k, cb) for k in (S_CA, S_HA, S_BA, S_ZA)])
    pga = proj_dots(S_GA)
    pgb = proj_dots(S_GB)
    for cb in blocks:
        merge_gates(cb, pga[cb], pgb[cb])
    yb_buf[...] = _dot(yb_in[...], w_pb[...])
    ya_buf[...] = _dot(ya_in[...], w_pa[...])

    _pre_norm(x_next, xn_next, p_ref)


def _pack_kernel(*refs):
    n = len(refs) // 2
    for w_ref, o_ref in zip(refs[:n], refs[n:]):
        o_ref[...] = pltpu.bitcast(w_ref[...].astype(BF16), jnp.uint32)


def _pack_rows(*ws):
    k, n = ws[0].shape
    assert all(w.shape == (k, n) for w in ws)
    bk = min(k, PACK_BLOCK_ROWS)
    bn = min(n, PACK_BLOCK_COLS)
    assert k % bk == 0 and n % bn == 0
    return pl.pallas_call(
        _pack_kernel,
        grid=(k // bk, n // bn),
        in_specs=[pl.BlockSpec((bk, bn), lambda i, j: (i, j))] * len(ws),
        out_specs=[pl.BlockSpec((bk // 2, bn), lambda i, j: (i, j))] * len(ws),
        out_shape=[jax.ShapeDtypeStruct((k // 2, n), jnp.uint32)] * len(ws),
        name="pack_weights",
    )(*ws)


def _const_spec(shape):
    nd = len(shape)
    return pl.BlockSpec(shape, lambda *_: (0,) * nd, pipeline_mode=pl.Buffered(1))


def kernel(x, meta, norm_g, w_in, b_gate, conv_a_w, w_proj_a, conv_b_w, conv_b_b, w_rg_a, b_rg_a, w_rg_i, b_rg_i, lru_param, w_proj_b, w_out, final_norm_g):
    bsz, seq, d = x.shape
    assert norm_g.shape[0] == 1, "single-layer trunk only"
    assert bsz % NB == 0 and seq % TT == 0 and N_META % SLAB_T == 0
    assert w_in.shape[2] == N_SPLITS * d and d == D_MODEL and CB % (d // N_LRU_HEADS) == 0

    rows = [norm_g[0], *conv_a_w[0], *conv_b_w[0], conv_b_b[0], b_rg_a[0].reshape(d), b_rg_i[0].reshape(d),
            lru_param[0], b_gate[0, :d], b_gate[0, d:], final_norm_g]
    params = jnp.broadcast_to(jnp.stack(rows)[:, None, :], (N_PARAM_ROWS, SLAB, d)).astype(F32)
    hd = d // N_LRU_HEADS
    (w_in_b,) = _pack_rows(w_in[0])
    w_pa_b, w_pb_b, w_o_b = _pack_rows(w_proj_a[0], w_proj_b[0], w_out[0])
    w_ra_b, w_ri_b = (w.reshape(N_LRU_HEADS, hd // 2, hd)
                      for w in _pack_rows(w_rg_a[0].reshape(d, hd), w_rg_i[0].reshape(d, hd)))
    xm = jnp.repeat(meta.astype(F32), NB, axis=0)

    mm = N_META * NB
    col = lambda k: pl.BlockSpec((d // 2, d), lambda i, k=k: (0, k))
    whole = lambda shape: pl.BlockSpec(shape, lambda i: (0,) * len(shape))
    st_a, st_b, st_h = pl.pallas_call(
        _meta_kernel,
        grid=(1,),
        in_specs=[whole((mm, d)), whole(params.shape), col(S_CA), col(S_HA), col(S_XB),
                  whole(w_ra_b.shape), whole(w_ri_b.shape)],
        out_specs=[whole((SLAB, d)), whole((3 * NB, d)), whole((NB, d))],
        out_shape=[jax.ShapeDtypeStruct((SLAB, d), F32), jax.ShapeDtypeStruct((3 * NB, d), F32),
                   jax.ShapeDtypeStruct((NB, d), F32)],
        scratch_shapes=[pltpu.VMEM((mm, d), BF16), pltpu.VMEM((mm, d), F32), pltpu.VMEM((mm, d), BF16)],
        name="meta_state",
    )(xm, params, w_in_b, w_in_b, w_in_b, w_ra_b, w_ri_b)

    m = TT * NB
    n_chunks = (bsz // NB) * (seq // TT)
    assert n_chunks >= N_XBUF and (n_chunks + TAIL_LAG) % UNROLL == 0
    act_f32 = pltpu.VMEM((m, d), F32)
    act_bf16 = pltpu.VMEM((m, d), BF16)
    return pl.pallas_call(
        _main_kernel,
        grid=((n_chunks + TAIL_LAG) // UNROLL,),
        in_specs=[pl.BlockSpec(memory_space=pl.ANY),
                  _const_spec((SLAB, d)), _const_spec((3 * NB, d)), _const_spec((NB, d)),
                  _const_spec(params.shape), _const_spec(w_in_b.shape), _const_spec(w_pa_b.shape),
                  _const_spec(w_pb_b.shape), _const_spec(w_o_b.shape), _const_spec(w_ra_b.shape),
                  _const_spec(w_ri_b.shape)],
        out_specs=pl.BlockSpec(memory_space=pl.ANY),
        out_shape=jax.ShapeDtypeStruct((bsz, seq, d), x.dtype),
        scratch_shapes=[
            pltpu.VMEM((N_XBUF, TT, NB, d), F32),
            pltpu.VMEM((N_OBUF, TT, NB, d), F32),
            pltpu.SemaphoreType.DMA((N_XBUF,)),
            pltpu.SemaphoreType.DMA((N_OBUF,)),
            pltpu.VMEM((SLAB, d), F32),
            pltpu.VMEM((3 * NB, d), F32),
            pltpu.VMEM((NB, d), F32),
            pltpu.VMEM((N_XNBUF, m, d), BF16),
            act_f32, act_bf16,
            act_f32, act_f32, act_f32, act_f32,
            act_bf16, act_bf16, act_bf16,
        ],
        compiler_params=pltpu.CompilerParams(
            dimension_semantics=("arbitrary",),
            vmem_limit_bytes=V7X_VMEM_BYTES * 7 // 8,
        ),
        name="mixer_block",
    )(x, st_a, st_b, st_h, params, w_in_b, w_pa_b, w_pb_b, w_o_b, w_ra_b, w_ri_b)
```

```python
import jax
import jax.numpy as jnp
from jax import lax
from jax.experimental import pallas as pl
from jax.experimental.pallas import tpu as pltpu

N_META = 16
N_LRU_HEADS = 4
C_LRU = 8.0
RMS_EPS = 1e-6
LOG2_E = 1.4426950408889634
SQRT_GUARD = 1e-30

V7X_F32_SUBLANES = 8
V7X_VMEM_BYTES = 64 * 1024 * 1024

NB = V7X_F32_SUBLANES
SLAB_T = 2
SLAB = SLAB_T * NB
TT = 32
UNROLL = 1
TAIL_LAG = 2
N_XBUF = 2 * UNROLL + 3
N_OBUF = 2 * UNROLL
N_XNBUF = 3
N_SPLITS = 8
D_MODEL = 1024
CB = 512
N_COL_BLOCKS = D_MODEL // CB
PACK_BLOCK_ROWS = 512
PACK_BLOCK_COLS = 2048

P_NORM_G, P_CONV_A, P_CONV_B, P_CONV_B_BIAS = 0, 1, 4, 8
P_B_RG_A, P_B_RG_I, P_LAM, P_BGATE_A, P_BGATE_B, P_FINAL_G = 9, 10, 11, 12, 13, 14
N_PARAM_ROWS = 15

S_BA, S_CA, S_HA, S_ZA, S_XB, S_ZB, S_GA, S_GB = range(N_SPLITS)

F32 = jnp.float32
BF16 = jnp.bfloat16


def _sigmoid(v):
    return 1.0 / (1.0 + jnp.exp2(v * -LOG2_E))


def _rms_norm(v, g):
    ms = jnp.mean(v * v, axis=-1, keepdims=True)
    return v * lax.rsqrt(ms + RMS_EPS) * g


def _param(p_ref, row, cols=slice(None)):
    p = p_ref[row, :, cols]
    return jnp.concatenate([p] * SLAB_T, axis=0)


def _slabs(n_rows):
    return [slice(j * SLAB, (j + 1) * SLAB) for j in range(n_rows // SLAB)]


def _dot(a, w_words):
    return jnp.dot(a, pltpu.bitcast(w_words, BF16), preferred_element_type=F32)


def _cols(cb):
    return slice(cb * CB, (cb + 1) * CB)


def _in_proj_cols(xn, w_in, k, cb):
    d = xn.shape[1]
    return _dot(xn, w_in[:, k * d + cb * CB:k * d + (cb + 1) * CB])


def _conv_b_block(px, hist_b, p_ref, cols, xc_buf, xcb_buf):
    cw = [_param(p_ref, P_CONV_B + k, cols) for k in range(4)]
    bias = _param(p_ref, P_CONV_B_BIAS, cols)
    p2h = hist_b[:NB, cols]
    p1 = hist_b[NB:, cols]
    for sl in _slabs(px.shape[0]):
        xb = px[sl]
        tap0 = jnp.concatenate([p2h, p1[:NB]], axis=0)
        tap2 = jnp.concatenate([p1[NB:], xb[:NB]], axis=0)
        xc = cw[0] * tap0 + cw[1] * p1 + cw[2] * tap2 + cw[3] * xb + bias
        xc_buf[sl, cols] = xc
        xcb_buf[sl, cols] = xc.astype(BF16)
        p2h = p1[NB:]
        p1 = xb
    hist_b[:NB, cols] = p2h
    hist_b[NB:, cols] = p1


def _gate_dots(xcb_buf, w_ra, w_ri, cb):
    hd = w_ra.shape[2]
    heads = range(cb * (CB // hd), (cb + 1) * (CB // hd))
    parts = [xcb_buf[:, h * hd:(h + 1) * hd] for h in heads]
    ri = [_dot(p, jnp.concatenate([w_ra[h], w_ri[h]], axis=1)) for p, h in zip(parts, heads)]
    r = jnp.concatenate([x[:, :hd] for x in ri], axis=1)
    i = jnp.concatenate([x[:, hd:] for x in ri], axis=1)
    return r, i


def _lru_block(r, i, xc_buf, p_ref, hstate, cols, first, pzb=None, yb_in=None):
    b_ra = _param(p_ref, P_B_RG_A, cols)
    b_ri = _param(p_ref, P_B_RG_I, cols)
    z = -p_ref[P_LAM, :, cols]
    softplus = jnp.maximum(z, 0.0) + jnp.log1p(jnp.exp(-jnp.abs(z)))
    cs = jnp.concatenate([(-C_LRU * LOG2_E) * softplus] * SLAB_T, axis=0)
    h = hstate[:, cols]
    for j, sl in enumerate(_slabs(r.shape[0])):
        rg = _sigmoid(r[sl] + b_ra)
        ig = _sigmoid(i[sl] + b_ri)
        a = jnp.exp2(cs * rg)
        y = 1.0 - a * a
        mult = y * lax.rsqrt(jnp.maximum(y, SQRT_GUARD))
        if first and j == 0:
            mult = jnp.concatenate([jnp.ones((NB, mult.shape[1]), F32), mult[NB:]], axis=0)
        u = xc_buf[sl, cols] * ig * mult
        h0 = a[:NB] * h + u[:NB]
        h = a[NB:] * h0 + u[NB:]
        if yb_in is not None:
            zb = pzb[sl]
            hs = jnp.concatenate([h0, h], axis=0)
            yb_in[sl, cols] = (hs * (zb * _sigmoid(zb))).astype(BF16)
    hstate[:, cols] = h


def _meta_kernel(xm_ref, p_ref, w_ca, w_ha, w_xb, w_ra, w_ri,
                 sta_ref, stb_ref, sth_ref, xn_buf, xc_buf, xcb_buf):
    m = xm_ref.shape[0]
    g_norm = _param(p_ref, P_NORM_G)
    for sl in _slabs(m):
        xn_buf[sl, :] = _rms_norm(xm_ref[sl, :], g_norm).astype(BF16)
    xn = xn_buf[...]
    last = slice(m - SLAB, m)
    stb_ref[...] = jnp.zeros_like(stb_ref)
    sth_ref[...] = jnp.zeros_like(sth_ref)
    for cb in range(N_COL_BLOCKS):
        cols = _cols(cb)
        sta_ref[:, cols] = _dot(xn, w_ca[:, cols])[last] * _dot(xn, w_ha[:, cols])[last]
        _conv_b_block(_dot(xn, w_xb[:, cols]), stb_ref, p_ref, cols, xc_buf, xcb_buf)
        r, i = _gate_dots(xcb_buf, w_ra, w_ri, cb)
        _lru_block(r, i, xc_buf, p_ref, sth_ref, cols, first=True)


def _main_kernel(x_hbm, sta_ref, stb_ref, sth_ref, p_ref, w_in, w_pa, w_pb, w_o, w_ra, w_ri,
                 o_hbm,
                 xbuf, obuf, sem_in, sem_out, hist_a, hist_b, hstate,
                 xn_buf, xc_buf, xcb_buf, ya_buf, yb_buf, ga_buf, gb_buf, ya_in, yb_in, m_buf):
    step = pl.program_id(0)
    n = (x_hbm.shape[0] // NB) * (x_hbm.shape[1] // TT)
    nc = x_hbm.shape[1] // TT
    q0 = step * UNROLL

    def in_copies(j):
        gg, cc, slot = lax.div(j, nc), lax.rem(j, nc), lax.rem(j, N_XBUF)
        return [pltpu.make_async_copy(x_hbm.at[gg * NB + b, pl.ds(cc * TT, TT), :],
                                      xbuf.at[slot, :, b, :], sem_in.at[slot]) for b in range(NB)]

    def out_copies(j):
        gg, cc, slot = lax.div(j, nc), lax.rem(j, nc), lax.rem(j, N_OBUF)
        return [pltpu.make_async_copy(obuf.at[slot, :, b, :],
                                      o_hbm.at[gg * NB + b, pl.ds(cc * TT, TT), :], sem_out.at[slot]) for b in range(NB)]

    def guarded(cond, copies_of, j, start):
        @pl.when(cond)
        def _():
            for cp in copies_of(j):
                cp.start() if start else cp.wait()

    @pl.when(step == 0)
    def _():
        for j in range(UNROLL + 1):
            for cp in in_copies(j):
                cp.start()
        for slot in range(N_XBUF - TAIL_LAG, N_XBUF):
            xbuf[slot] = jnp.zeros(xbuf.shape[1:], F32)
        xn_buf[N_XNBUF - 1] = jnp.zeros(xn_buf.shape[1:], BF16)
        for ref in (xc_buf, xcb_buf, ya_buf, yb_buf, ga_buf, gb_buf, hist_a, hist_b, hstate):
            ref[...] = jnp.zeros_like(ref)
        for cp in in_copies(0):
            cp.wait()
        _pre_norm(xbuf.at[0], xn_buf.at[0], p_ref)

    for u in range(UNROLL):
        guarded(q0 + UNROLL + 1 + u < n, in_copies, q0 + UNROLL + 1 + u, start=True)
    for u in range(UNROLL):
        guarded(q0 + 1 + u < n, in_copies, q0 + 1 + u, start=False)
    for u in range(UNROLL):
        guarded(q0 + u >= TAIL_LAG + N_OBUF, out_copies, q0 + u - TAIL_LAG - N_OBUF, start=False)

    for u in range(UNROLL):
        _pipeline_position(q0 + u, nc, sta_ref, stb_ref, sth_ref, p_ref, w_in, w_pa, w_pb, w_o, w_ra, w_ri,
                           xbuf, obuf, hist_a, hist_b, hstate,
                           xn_buf, xc_buf, xcb_buf, ya_buf, yb_buf, ga_buf, gb_buf, ya_in, yb_in, m_buf)

    for u in range(UNROLL):
        guarded(q0 + u >= TAIL_LAG, out_copies, q0 + u - TAIL_LAG, start=True)

    @pl.when(step == pl.num_programs(0) - 1)
    def _():
        for j in range(n - N_OBUF, n):
            for cp in out_copies(j):
                cp.wait()


def _x_slab(x_ref, j):
    return jnp.concatenate([x_ref[j * SLAB_T + t] for t in range(SLAB_T)], axis=0)


def _pre_norm(x_ref, xn_ref, p_ref):
    g_norm = _param(p_ref, P_NORM_G)
    for j, sl in enumerate(_slabs(TT * NB)):
        xn_ref[sl, :] = _rms_norm(_x_slab(x_ref, j), g_norm).astype(BF16)


def _pipeline_position(q, nc, sta_ref, stb_ref, sth_ref, p_ref, w_in, w_pa, w_pb, w_o, w_ra, w_ri,
                       xbuf, obuf, hist_a, hist_b, hstate,
                       xn_buf, xc_buf, xcb_buf, ya_buf, yb_buf, ga_buf, gb_buf, ya_in, yb_in, m_buf):
    slabs = _slabs(TT * NB)
    blocks = range(N_COL_BLOCKS)

    front_first = lax.rem(q, nc) == 0
    body_first = lax.rem(q + nc - 1, nc) == 0
    hist_b[...] = jnp.where(front_first, stb_ref[...], hist_b[...])
    hist_a[...] = jnp.where(body_first, sta_ref[...], hist_a[...])
    hstate[...] = jnp.where(body_first, sth_ref[...], hstate[...])

    x_tail = xbuf.at[lax.rem(q + (N_XBUF - 1) * TAIL_LAG, N_XBUF)]
    x_next = xbuf.at[lax.rem(q + 1, N_XBUF)]
    o_tail = obuf.at[lax.rem(q + (N_OBUF - 1) * TAIL_LAG, N_OBUF)]
    xn_front = xn_buf[lax.rem(q, N_XNBUF)]
    xn = xn_buf[lax.rem(q + N_XNBUF - 1, N_XNBUF)]
    xn_next = xn_buf.at[lax.rem(q + 1, N_XNBUF)]

    def proj_dots(k):
        return [_in_proj_cols(xn, w_in, k, cb) for cb in blocks]

    def branch_a(cb, pc, ph, pb, pz):
        cols = _cols(cb)
        cw = [_param(p_ref, P_CONV_A + k, cols) for k in range(3)]
        prev = hist_a[:, cols]
        for sl in slabs:
            ch = pc[sl] * ph[sl]
            tap1 = jnp.concatenate([prev[NB:], ch[:NB]], axis=0)
            conv = cw[0] * prev + cw[1] * tap1 + cw[2] * ch
            za = pz[sl]
            ya_in[sl, cols] = (pb[sl] * conv * (za * _sigmoid(za))).astype(BF16)
            prev = ch
        hist_a[:, cols] = prev

    def merge_gates(cb, pga, pgb):
        cols = _cols(cb)
        bga = _param(p_ref, P_BGATE_A, cols)
        bgb = _param(p_ref, P_BGATE_B, cols)
        for sl in slabs:
            ga_buf[sl, cols] = _sigmoid(pga[sl] + bga)
            gb_buf[sl, cols] = _sigmoid(pgb[sl] + bgb)

    for sl in slabs:
        m_buf[sl, :] = (ga_buf[sl, :] * ya_buf[sl, :] + gb_buf[sl, :] * yb_buf[sl, :]).astype(BF16)

    gates = [_gate_dots(xcb_buf, w_ra, w_ri, cb) for cb in blocks]
    pzb = proj_dots(S_ZB)
    o = _dot(m_buf[...], w_o[...])
    for cb in blocks:
        r, i = gates[cb]
        _lru_block(r, i, xc_buf, p_ref, hstate, _cols(cb), first=False, pzb=pzb[cb], yb_in=yb_in)

    for cb in blocks:
        _conv_b_block(_in_proj_cols(xn_front, w_in, S_XB, cb), hist_b, p_ref, _cols(cb), xc_buf, xcb_buf)

    g_fin = _param(p_ref, P_FINAL_G)
    for j, sl in enumerate(slabs):
        out = _rms_norm(_x_slab(x_tail, j) + o[sl], g_fin)
        for t in range(SLAB_T):
            o_tail[j * SLAB_T + t] = out[t * NB:(t + 1) * NB]

    for cb in blocks:
        branch_a(cb, *[_in_proj_cols(xn, w_in, k, cb) for k in (S_CA, S_HA, S_BA, S_ZA)])
    pga = proj_dots(S_GA)
    pgb = proj_dots(S_GB)
    for cb in blocks:
        merge_gates(cb, pga[cb], pgb[cb])
    yb_buf[...] = _dot(yb_in[...], w_pb[...])
    ya_buf[...] = _dot(ya_in[...], w_pa[...])

    _pre_norm(x_next, xn_next, p_ref)


def _pack_kernel(*refs):
    n = len(refs) // 2
    for w_ref, o_ref in zip(refs[:n], refs[n:]):
        o_ref[...] = pltpu.bitcast(w_ref[...].astype(BF16), jnp.uint32)


def _pack_rows(*ws):
    k, n = ws[0].shape
    assert all(w.shape == (k, n) for w in ws)
    bk = min(k, PACK_BLOCK_ROWS)
    bn = min(n, PACK_BLOCK_COLS)
    assert k % bk == 0 and n % bn == 0
    return pl.pallas_call(
        _pack_kernel,
        grid=(k // bk, n // bn),
        in_specs=[pl.BlockSpec((bk, bn), lambda i, j: (i, j))] * len(ws),
        out_specs=[pl.BlockSpec((bk // 2, bn), lambda i, j: (i, j))] * len(ws),
        out_shape=[jax.ShapeDtypeStruct((k // 2, n), jnp.uint32)] * len(ws),
        name="pack_weights",
    )(*ws)


def _const_spec(shape):
    nd = len(shape)
    return pl.BlockSpec(shape, lambda *_: (0,) * nd, pipeline_mode=pl.Buffered(1))


def kernel(x, meta, norm_g, w_in, b_gate, conv_a_w, w_proj_a, conv_b_w, conv_b_b, w_rg_a, b_rg_a, w_rg_i, b_rg_i, lru_param, w_proj_b, w_out, final_norm_g):
    bsz, seq, d = x.shape
    assert norm_g.shape[0] == 1, "single-layer trunk only"
    assert bsz % NB == 0 and seq % TT == 0 and N_META % SLAB_T == 0
    assert w_in.shape[2] == N_SPLITS * d and d == D_MODEL and CB % (d // N_LRU_HEADS) == 0

    rows = [norm_g[0], *conv_a_w[0], *conv_b_w[0], conv_b_b[0], b_rg_a[0].reshape(d), b_rg_i[0].reshape(d),
            lru_param[0], b_gate[0, :d], b_gate[0, d:], final_norm_g]
    params = jnp.broadcast_to(jnp.stack(rows)[:, None, :], (N_PARAM_ROWS, NB, d)).astype(F32)
    hd = d // N_LRU_HEADS
    (w_in_b,) = _pack_rows(w_in[0])
    w_pa_b, w_pb_b, w_o_b = _pack_rows(w_proj_a[0], w_proj_b[0], w_out[0])
    w_ra_b, w_ri_b = (w.reshape(N_LRU_HEADS, hd // 2, hd)
                      for w in _pack_rows(w_rg_a[0].reshape(d, hd), w_rg_i[0].reshape(d, hd)))
    xm = jnp.repeat(meta.astype(F32), NB, axis=0)

    mm = N_META * NB
    col = lambda k: pl.BlockSpec((d // 2, d), lambda i, k=k: (0, k))
    whole = lambda shape: pl.BlockSpec(shape, lambda i: (0,) * len(shape))
    st_a, st_b, st_h = pl.pallas_call(
        _meta_kernel,
        grid=(1,),
        in_specs=[whole((mm, d)), whole(params.shape), col(S_CA), col(S_HA), col(S_XB),
                  whole(w_ra_b.shape), whole(w_ri_b.shape)],
        out_specs=[whole((SLAB, d)), whole((3 * NB, d)), whole((NB, d))],
        out_shape=[jax.ShapeDtypeStruct((SLAB, d), F32), jax.ShapeDtypeStruct((3 * NB, d), F32),
                   jax.ShapeDtypeStruct((NB, d), F32)],
        scratch_shapes=[pltpu.VMEM((mm, d), BF16), pltpu.VMEM((mm, d), F32), pltpu.VMEM((mm, d), BF16)],
        name="meta_state",
    )(xm, params, w_in_b, w_in_b, w_in_b, w_ra_b, w_ri_b)

    m = TT * NB
    n_chunks = (bsz // NB) * (seq // TT)
    assert n_chunks >= N_XBUF and (n_chunks + TAIL_LAG) % UNROLL == 0
    act_f32 = pltpu.VMEM((m, d), F32)
    act_bf16 = pltpu.VMEM((m, d), BF16)
    return pl.pallas_call(
        _main_kernel,
        grid=((n_chunks + TAIL_LAG) // UNROLL,),
        in_specs=[pl.BlockSpec(memory_space=pl.ANY),
                  _const_spec((SLAB, d)), _const_spec((3 * NB, d)), _const_spec((NB, d)),
                  _const_spec(params.shape), _const_spec(w_in_b.shape), _const_spec(w_pa_b.shape),
                  _const_spec(w_pb_b.shape), _const_spec(w_o_b.shape), _const_spec(w_ra_b.shape),
                  _const_spec(w_ri_b.shape)],
        out_specs=pl.BlockSpec(memory_space=pl.ANY),
        out_shape=jax.ShapeDtypeStruct((bsz, seq, d), x.dtype),
        scratch_shapes=[
            pltpu.VMEM((N_XBUF, TT, NB, d), F32),
            pltpu.VMEM((N_OBUF, TT, NB, d), F32),
            pltpu.SemaphoreType.DMA((N_XBUF,)),
            pltpu.SemaphoreType.DMA((N_OBUF,)),
            pltpu.VMEM((SLAB, d), F32),
            pltpu.VMEM((3 * NB, d), F32),
            pltpu.VMEM((NB, d), F32),
            pltpu.VMEM((N_XNBUF, m, d), BF16),
            act_f32, act_bf16,
            act_f32, act_f32, act_f32, act_f32,
            act_bf16, act_bf16, act_bf16,
        ],
        compiler_params=pltpu.CompilerParams(
            dimension_semantics=("arbitrary",),
            vmem_limit_bytes=V7X_VMEM_BYTES * 7 // 8,
        ),
        name="mixer_block",
    )(x, st_a, st_b, st_h, params, w_in_b, w_pa_b, w_pb_b, w_o_b, w_ra_b, w_ri_b)
```

```python
import jax
import jax.numpy as jnp
from jax import lax
from jax.experimental import pallas as pl
from jax.experimental.pallas import tpu as pltpu

N_META = 16
N_LRU_HEADS = 4
C_LRU = 8.0
RMS_EPS = 1e-6
LOG2_E = 1.4426950408889634
SQRT_GUARD = 1e-30

V7X_F32_SUBLANES = 8
V7X_VMEM_BYTES = 64 * 1024 * 1024

NB = V7X_F32_SUBLANES
SLAB_T = 2
SLAB = SLAB_T * NB
TT = 32
UNROLL = 1
TAIL_LAG = 2
N_XBUF = 2 * UNROLL + 3
N_OBUF = 2 * UNROLL
N_XNBUF = 3
N_SPLITS = 8
D_MODEL = 1024
CB = 512
N_COL_BLOCKS = D_MODEL // CB
PACK_BLOCK_ROWS = 256
PACK_BLOCK_COLS = 2048
N_PACK_STAGES = 4

P_NORM_G, P_CONV_A, P_CONV_B, P_CONV_B_BIAS = 0, 1, 4, 8
P_B_RG_A, P_B_RG_I, P_LAM, P_BGATE_A, P_BGATE_B, P_FINAL_G = 9, 10, 11, 12, 13, 14
N_PARAM_ROWS = 15

S_BA, S_CA, S_HA, S_ZA, S_XB, S_ZB, S_GA, S_GB = range(N_SPLITS)

F32 = jnp.float32
BF16 = jnp.bfloat16


def _sigmoid(v):
    return 1.0 / (1.0 + jnp.exp2(v * -LOG2_E))


def _rms_norm(v, g):
    ms = jnp.mean(v * v, axis=-1, keepdims=True)
    return v * lax.rsqrt(ms + RMS_EPS) * g


def _param(p_ref, row, cols=slice(None)):
    p = p_ref[row, :, cols]
    return jnp.concatenate([p] * SLAB_T, axis=0)


def _slabs(n_rows):
    return [slice(j * SLAB, (j + 1) * SLAB) for j in range(n_rows // SLAB)]


def _dot(a, w):
    wb = pltpu.bitcast(w, BF16) if w.dtype == jnp.uint32 else w.astype(BF16)
    return jnp.dot(a, wb, preferred_element_type=F32)


def _row_packing(w_ref):
    return 2 if w_ref.dtype == jnp.uint32 else 1


def _cols(cb):
    return slice(cb * CB, (cb + 1) * CB)


def _in_proj_cols(xn, w_in, k, cb):
    d = xn.shape[1]
    return _dot(xn, w_in[:, k * d + cb * CB:k * d + (cb + 1) * CB])


def _conv_b_block(px, hist_b, p_ref, cols, xc_buf, xcb_buf):
    cw = [_param(p_ref, P_CONV_B + k, cols) for k in range(4)]
    bias = _param(p_ref, P_CONV_B_BIAS, cols)
    p2h = hist_b[:NB, cols]
    p1 = hist_b[NB:, cols]
    for sl in _slabs(px.shape[0]):
        xb = px[sl]
        tap0 = jnp.concatenate([p2h, p1[:NB]], axis=0)
        tap2 = jnp.concatenate([p1[NB:], xb[:NB]], axis=0)
        xc = cw[0] * tap0 + cw[1] * p1 + cw[2] * tap2 + cw[3] * xb + bias
        xc_buf[sl, cols] = xc
        xcb_buf[sl, cols] = xc.astype(BF16)
        p2h = p1[NB:]
        p1 = xb
    hist_b[:NB, cols] = p2h
    hist_b[NB:, cols] = p1


def _gate_dots(xcb_buf, w_ra, w_ri, cb):
    hd = w_ra.shape[1]
    hr = hd // _row_packing(w_ra)
    heads = range(cb * (CB // hd), (cb + 1) * (CB // hd))
    parts = [xcb_buf[:, h * hd:(h + 1) * hd] for h in heads]
    ri = [_dot(p, jnp.concatenate([w_ra[h * hr:(h + 1) * hr, :], w_ri[h * hr:(h + 1) * hr, :]], axis=1))
          for p, h in zip(parts, heads)]
    r = jnp.concatenate([x[:, :hd] for x in ri], axis=1)
    i = jnp.concatenate([x[:, hd:] for x in ri], axis=1)
    return r, i


def _lru_block(r, i, xc_buf, p_ref, hstate, cols, first, pzb=None, yb_in=None):
    b_ra = _param(p_ref, P_B_RG_A, cols)
    b_ri = _param(p_ref, P_B_RG_I, cols)
    z = -p_ref[P_LAM, :, cols]
    softplus = jnp.maximum(z, 0.0) + jnp.log1p(jnp.exp(-jnp.abs(z)))
    cs = jnp.concatenate([(-C_LRU * LOG2_E) * softplus] * SLAB_T, axis=0)
    h = hstate[:, cols]
    for j, sl in enumerate(_slabs(r.shape[0])):
        rg = _sigmoid(r[sl] + b_ra)
        ig = _sigmoid(i[sl] + b_ri)
        a = jnp.exp2(cs * rg)
        y = 1.0 - a * a
        mult = y * lax.rsqrt(jnp.maximum(y, SQRT_GUARD))
        if first and j == 0:
            mult = jnp.concatenate([jnp.ones((NB, mult.shape[1]), F32), mult[NB:]], axis=0)
        u = xc_buf[sl, cols] * ig * mult
        h0 = a[:NB] * h + u[:NB]
        h = a[NB:] * h0 + u[NB:]
        if yb_in is not None:
            zb = pzb[sl]
            hs = jnp.concatenate([h0, h], axis=0)
            yb_in[sl, cols] = (hs * (zb * _sigmoid(zb))).astype(BF16)
    hstate[:, cols] = h


def _meta_kernel(xm_ref, p_ref, w_ca, w_ha, w_xb, w_ra, w_ri,
                 sta_ref, stb_ref, sth_ref, xn_buf, xc_buf, xcb_buf):
    m = xm_ref.shape[0]
    g_norm = _param(p_ref, P_NORM_G)
    for sl in _slabs(m):
        xn_buf[sl, :] = _rms_norm(xm_ref[sl, :], g_norm).astype(BF16)
    xn = xn_buf[...]
    last = slice(m - SLAB, m)
    stb_ref[...] = jnp.zeros_like(stb_ref)
    sth_ref[...] = jnp.zeros_like(sth_ref)
    for cb in range(N_COL_BLOCKS):
        cols = _cols(cb)
        sta_ref[:, cols] = _dot(xn, w_ca[:, cols])[last] * _dot(xn, w_ha[:, cols])[last]
        _conv_b_block(_dot(xn, w_xb[:, cols]), stb_ref, p_ref, cols, xc_buf, xcb_buf)
        r, i = _gate_dots(xcb_buf, w_ra, w_ri, cb)
        _lru_block(r, i, xc_buf, p_ref, sth_ref, cols, first=True)


def _main_kernel(x_hbm, sta_ref, stb_ref, sth_ref, p_ref, w_in_hbm, w_pa_hbm, w_pb_hbm, w_o_hbm, w_ra_hbm, w_ri_hbm,
                 o_hbm,
                 w_in, w_pa, w_pb, w_o, w_ra, w_ri, w_stage, sem_w,
                 xbuf, obuf, sem_in, sem_out, hist_a, hist_b, hstate,
                 xn_buf, xc_buf, xcb_buf, ya_buf, yb_buf, ga_buf, gb_buf, ya_in, yb_in, m_buf):
    step = pl.program_id(0)
    n = (x_hbm.shape[0] // NB) * (x_hbm.shape[1] // TT)
    nc = x_hbm.shape[1] // TT
    q0 = step * UNROLL

    def in_copies(j):
        gg, cc, slot = lax.div(j, nc), lax.rem(j, nc), lax.rem(j, N_XBUF)
        return [pltpu.make_async_copy(x_hbm.at[gg * NB + b, pl.ds(cc * TT, TT), :],
                                      xbuf.at[slot, :, b, :], sem_in.at[slot]) for b in range(NB)]

    def out_copies(j):
        gg, cc, slot = lax.div(j, nc), lax.rem(j, nc), lax.rem(j, N_OBUF)
        return [pltpu.make_async_copy(obuf.at[slot, :, b, :],
                                      o_hbm.at[gg * NB + b, pl.ds(cc * TT, TT), :], sem_out.at[slot]) for b in range(NB)]

    def guarded(cond, copies_of, j, start):
        @pl.when(cond)
        def _():
            for cp in copies_of(j):
                cp.start() if start else cp.wait()

    @pl.when(step == 0)
    def _():
        for j in range(UNROLL + 1):
            for cp in in_copies(j):
                cp.start()
        _pack_into([(w_in_hbm, w_in), (w_pa_hbm, w_pa), (w_pb_hbm, w_pb), (w_o_hbm, w_o),
                    (w_ra_hbm, w_ra), (w_ri_hbm, w_ri)], w_stage, sem_w)
        for slot in range(N_XBUF - TAIL_LAG, N_XBUF):
            xbuf[slot] = jnp.zeros(xbuf.shape[1:], F32)
        xn_buf[N_XNBUF - 1] = jnp.zeros(xn_buf.shape[1:], BF16)
        for ref in (xc_buf, xcb_buf, ya_buf, yb_buf, ga_buf, gb_buf, hist_a, hist_b, hstate):
            ref[...] = jnp.zeros_like(ref)
        for cp in in_copies(0):
            cp.wait()
        _pre_norm(xbuf.at[0], xn_buf.at[0], p_ref)

    for u in range(UNROLL):
        guarded(q0 + UNROLL + 1 + u < n, in_copies, q0 + UNROLL + 1 + u, start=True)
    for u in range(UNROLL):
        guarded(q0 + 1 + u < n, in_copies, q0 + 1 + u, start=False)
    for u in range(UNROLL):
        guarded(q0 + u >= TAIL_LAG + N_OBUF, out_copies, q0 + u - TAIL_LAG - N_OBUF, start=False)

    for u in range(UNROLL):
        _pipeline_position(q0 + u, nc, sta_ref, stb_ref, sth_ref, p_ref, w_in, w_pa, w_pb, w_o, w_ra, w_ri,
                           xbuf, obuf, hist_a, hist_b, hstate,
                           xn_buf, xc_buf, xcb_buf, ya_buf, yb_buf, ga_buf, gb_buf, ya_in, yb_in, m_buf)

    for u in range(UNROLL):
        guarded(q0 + u >= TAIL_LAG, out_copies, q0 + u - TAIL_LAG, start=True)

    @pl.when(step == pl.num_programs(0) - 1)
    def _():
        for j in range(n - N_OBUF, n):
            for cp in out_copies(j):
                cp.wait()


def _x_slab(x_ref, j):
    return jnp.concatenate([x_ref[j * SLAB_T + t] for t in range(SLAB_T)], axis=0)


def _pre_norm(x_ref, xn_ref, p_ref):
    g_norm = _param(p_ref, P_NORM_G)
    for j, sl in enumerate(_slabs(TT * NB)):
        xn_ref[sl, :] = _rms_norm(_x_slab(x_ref, j), g_norm).astype(BF16)


def _pipeline_position(q, nc, sta_ref, stb_ref, sth_ref, p_ref, w_in, w_pa, w_pb, w_o, w_ra, w_ri,
                       xbuf, obuf, hist_a, hist_b, hstate,
                       xn_buf, xc_buf, xcb_buf, ya_buf, yb_buf, ga_buf, gb_buf, ya_in, yb_in, m_buf):
    slabs = _slabs(TT * NB)
    blocks = range(N_COL_BLOCKS)

    front_first = lax.rem(q, nc) == 0
    body_first = lax.rem(q + nc - 1, nc) == 0
    hist_b[...] = jnp.where(front_first, stb_ref[...], hist_b[...])
    hist_a[...] = jnp.where(body_first, sta_ref[...], hist_a[...])
    hstate[...] = jnp.where(body_first, sth_ref[...], hstate[...])

    x_tail = xbuf.at[lax.rem(q + (N_XBUF - 1) * TAIL_LAG, N_XBUF)]
    x_next = xbuf.at[lax.rem(q + 1, N_XBUF)]
    o_tail = obuf.at[lax.rem(q + (N_OBUF - 1) * TAIL_LAG, N_OBUF)]
    xn_front = xn_buf[lax.rem(q, N_XNBUF)]
    xn = xn_buf[lax.rem(q + N_XNBUF - 1, N_XNBUF)]
    xn_next = xn_buf.at[lax.rem(q + 1, N_XNBUF)]

    def proj_dots(k):
        return [_in_proj_cols(xn, w_in, k, cb) for cb in blocks]

    def branch_a(cb, pc, ph, pb, pz):
        cols = _cols(cb)
        cw = [_param(p_ref, P_CONV_A + k, cols) for k in range(3)]
        prev = hist_a[:, cols]
        for sl in slabs:
            ch = pc[sl] * ph[sl]
            tap1 = jnp.concatenate([prev[NB:], ch[:NB]], axis=0)
            conv = cw[0] * prev + cw[1] * tap1 + cw[2] * ch
            za = pz[sl]
            ya_in[sl, cols] = (pb[sl] * conv * (za * _sigmoid(za))).astype(BF16)
            prev = ch
        hist_a[:, cols] = prev

    def merge_gates(cb, pga, pgb):
        cols = _cols(cb)
        bga = _param(p_ref, P_BGATE_A, cols)
        bgb = _param(p_ref, P_BGATE_B, cols)
        for sl in slabs:
            ga_buf[sl, cols] = _sigmoid(pga[sl] + bga)
            gb_buf[sl, cols] = _sigmoid(pgb[sl] + bgb)

    for sl in slabs:
        m_buf[sl, :] = (ga_buf[sl, :] * ya_buf[sl, :] + gb_buf[sl, :] * yb_buf[sl, :]).astype(BF16)

    gates = [_gate_dots(xcb_buf, w_ra, w_ri, cb) for cb in blocks]
    pzb = proj_dots(S_ZB)
    o = _dot(m_buf[...], w_o[...])
    for cb in blocks:
        r, i = gates[cb]
        _lru_block(r, i, xc_buf, p_ref, hstate, _cols(cb), first=False, pzb=pzb[cb], yb_in=yb_in)

    for cb in blocks:
        _conv_b_block(_in_proj_cols(xn_front, w_in, S_XB, cb), hist_b, p_ref, _cols(cb), xc_buf, xcb_buf)

    g_fin = _param(p_ref, P_FINAL_G)
    for j, sl in enumerate(slabs):
        out = _rms_norm(_x_slab(x_tail, j) + o[sl], g_fin)
        for t in range(SLAB_T):
            o_tail[j * SLAB_T + t] = out[t * NB:(t + 1) * NB]

    for cb in blocks:
        branch_a(cb, *[_in_proj_cols(xn, w_in, k, cb) for k in (S_CA, S_HA, S_BA, S_ZA)])
    pga = proj_dots(S_GA)
    pgb = proj_dots(S_GB)
    for cb in blocks:
        merge_gates(cb, pga[cb], pgb[cb])
    yb_buf[...] = _dot(yb_in[...], w_pb[...])
    ya_buf[...] = _dot(ya_in[...], w_pa[...])

    _pre_norm(x_next, xn_next, p_ref)


def _pack_into(pairs, stage, sem):
    blocks = []
    for src, dst in pairs:
        k, n = src.shape
        bn = min(n, PACK_BLOCK_COLS)
        assert k % PACK_BLOCK_ROWS == 0 and n % bn == 0 and dst.shape == (k // 2, n)
        blocks += [(src, dst, r, c, bn) for r in range(0, k, PACK_BLOCK_ROWS) for c in range(0, n, bn)]

    def copy(i):
        src, _, r, c, bn = blocks[i]
        return pltpu.make_async_copy(src.at[pl.ds(r, PACK_BLOCK_ROWS), pl.ds(c, bn)],
                                     stage.at[i % stage.shape[0], :, pl.ds(0, bn)], sem.at[i % stage.shape[0]])

    ahead = stage.shape[0] - 1
    for i in range(min(ahead, len(blocks))):
        copy(i).start()
    for i, (_, dst, r, c, bn) in enumerate(blocks):
        if i + ahead < len(blocks):
            copy(i + ahead).start()
        copy(i).wait()
        words = pltpu.bitcast(stage[i % stage.shape[0], :, :bn].astype(BF16), jnp.uint32)
        dst[r // 2:(r + PACK_BLOCK_ROWS) // 2, c:c + bn] = words


def _const_spec(shape):
    nd = len(shape)
    return pl.BlockSpec(shape, lambda *_: (0,) * nd, pipeline_mode=pl.Buffered(1))


def kernel(x, meta, norm_g, w_in, b_gate, conv_a_w, w_proj_a, conv_b_w, conv_b_b, w_rg_a, b_rg_a, w_rg_i, b_rg_i, lru_param, w_proj_b, w_out, final_norm_g):
    bsz, seq, d = x.shape
    assert norm_g.shape[0] == 1, "single-layer trunk only"
    assert bsz % NB == 0 and seq % TT == 0 and N_META % SLAB_T == 0
    assert w_in.shape[2] == N_SPLITS * d and d == D_MODEL and CB % (d // N_LRU_HEADS) == 0

    rows = [norm_g[0], *conv_a_w[0], *conv_b_w[0], conv_b_b[0], b_rg_a[0].reshape(d), b_rg_i[0].reshape(d),
            lru_param[0], b_gate[0, :d], b_gate[0, d:], final_norm_g]
    params = jnp.broadcast_to(jnp.stack(rows)[:, None, :], (N_PARAM_ROWS, NB, d)).astype(F32)
    hd = d // N_LRU_HEADS
    w_in_f, w_pa_f, w_pb_f, w_o_f = w_in[0], w_proj_a[0], w_proj_b[0], w_out[0]
    w_ra_f, w_ri_f = w_rg_a[0].reshape(d, hd), w_rg_i[0].reshape(d, hd)
    xm = jnp.repeat(meta.astype(F32), NB, axis=0)

    mm = N_META * NB
    col = lambda k: pl.BlockSpec((d, d), lambda i, k=k: (0, k), pipeline_mode=pl.Buffered(1))
    whole = lambda shape: pl.BlockSpec(shape, lambda i: (0,) * len(shape))
    st_a, st_b, st_h = pl.pallas_call(
        _meta_kernel,
        grid=(1,),
        in_specs=[whole((mm, d)), whole(params.shape), col(S_CA), col(S_HA), col(S_XB),
                  whole(w_ra_f.shape), whole(w_ri_f.shape)],
        out_specs=[whole((SLAB, d)), whole((3 * NB, d)), whole((NB, d))],
        out_shape=[jax.ShapeDtypeStruct((SLAB, d), F32), jax.ShapeDtypeStruct((3 * NB, d), F32),
                   jax.ShapeDtypeStruct((NB, d), F32)],
        scratch_shapes=[pltpu.VMEM((mm, d), BF16), pltpu.VMEM((mm, d), F32), pltpu.VMEM((mm, d), BF16)],
        name="meta_state",
    )(xm, params, w_in_f, w_in_f, w_in_f, w_ra_f, w_ri_f)

    m = TT * NB
    n_chunks = (bsz // NB) * (seq // TT)
    assert n_chunks >= N_XBUF and (n_chunks + TAIL_LAG) % UNROLL == 0
    act_f32 = pltpu.VMEM((m, d), F32)
    act_bf16 = pltpu.VMEM((m, d), BF16)
    return pl.pallas_call(
        _main_kernel,
        grid=((n_chunks + TAIL_LAG) // UNROLL,),
        in_specs=[pl.BlockSpec(memory_space=pl.ANY),
                  _const_spec((SLAB, d)), _const_spec((3 * NB, d)), _const_spec((NB, d)),
                  _const_spec(params.shape)] + [pl.BlockSpec(memory_space=pl.ANY)] * 6,
        out_specs=pl.BlockSpec(memory_space=pl.ANY),
        out_shape=jax.ShapeDtypeStruct((bsz, seq, d), x.dtype),
        scratch_shapes=[
            pltpu.VMEM((d // 2, N_SPLITS * d), jnp.uint32),
            pltpu.VMEM((d // 2, d), jnp.uint32), pltpu.VMEM((d // 2, d), jnp.uint32),
            pltpu.VMEM((d // 2, d), jnp.uint32),
            pltpu.VMEM((d // 2, hd), jnp.uint32), pltpu.VMEM((d // 2, hd), jnp.uint32),
            pltpu.VMEM((N_PACK_STAGES, PACK_BLOCK_ROWS, PACK_BLOCK_COLS), F32),
            pltpu.SemaphoreType.DMA((N_PACK_STAGES,)),
            pltpu.VMEM((N_XBUF, TT, NB, d), F32),
            pltpu.VMEM((N_OBUF, TT, NB, d), F32),
            pltpu.SemaphoreType.DMA((N_XBUF,)),
            pltpu.SemaphoreType.DMA((N_OBUF,)),
            pltpu.VMEM((SLAB, d), F32),
            pltpu.VMEM((3 * NB, d), F32),
            pltpu.VMEM((NB, d), F32),
            pltpu.VMEM((N_XNBUF, m, d), BF16),
            act_f32, act_bf16,
            act_f32, act_f32, act_f32, act_f32,
            act_bf16, act_bf16, act_bf16,
        ],
        compiler_params=pltpu.CompilerParams(
            dimension_semantics=("arbitrary",),
            vmem_limit_bytes=V7X_VMEM_BYTES * 7 // 8,
        ),
        name="mixer_block",
    )(x, st_a, st_b, st_h, params, w_in_f, w_pa_f, w_pb_f, w_o_f, w_ra_f, w_ri_f)
```
